```python
import jax
import jax.numpy as jnp
from jax import lax
import numpy as np

D_MODEL = 2048
BATCH = 8
SEQ = 2048
DEPTH = 4
DEC_BATCH = 8
DEC_SEQ = 4096
PAST_LEN = 128

N_MIXERS = 3
N_MLA_LAYERS = (DEPTH + 2) // N_MIXERS
N_HGRN_LAYERS = (DEPTH + 1) // N_MIXERS
N_FNET_LAYERS = DEPTH // N_MIXERS

MLA_HEADS = 16
Q_LORA_RANK = 512
KV_LORA_RANK = 512
QK_NOPE_DIM = 128
QK_ROPE_DIM = 64
QK_HEAD_DIM = QK_NOPE_DIM + QK_ROPE_DIM
V_HEAD_DIM = 128
ROPE_THETA = 10000.0
Q_BLOCK = 128

HGRN_HEADS = 16
HGRN_KEY_DIM = 128
HGRN_VAL_DIM = D_MODEL // HGRN_HEADS
HGRN_KEY_WIDTH = HGRN_HEADS * HGRN_KEY_DIM
HGRN_CHUNK = 64

FNET_GROUPS = 4
FNET_GROUP_DIM = D_MODEL // FNET_GROUPS

MEM_TOKENS = 256
MEM_HEADS = 4
MEM_HEAD_DIM = 128

D_FF = 4 * D_MODEL
EPS = 1e-6

kernel_name = 'hybrid_mla_hgrn2_fnet_encoder'


def rms_norm(x, g):
    xf = x.astype(jnp.float32)
    y = xf * lax.rsqrt(jnp.mean(xf * xf, axis=-1, keepdims=True) + EPS)
    return (y * g.astype(jnp.float32)).astype(x.dtype)


def rope_cos_sin(length, dim):
    inv = 1.0 / (ROPE_THETA ** (jnp.arange(0, dim, 2, dtype=jnp.float32) / dim))
    ang = jnp.arange(length, dtype=jnp.float32)[:, None] * inv[None, :]
    return jnp.cos(ang), jnp.sin(ang)


def apply_rope(x, cos, sin):
    half = x.shape[-1] // 2
    xf = x.astype(jnp.float32)
    x1, x2 = xf[..., :half], xf[..., half:]
    c = cos[None, :, None, :]
    s = sin[None, :, None, :]
    return jnp.concatenate([x1 * c - x2 * s, x1 * s + x2 * c], axis=-1).astype(x.dtype)


def dense_bidirectional_attention(q, k, v, scale):
    B, L, H, Dk = q.shape
    nb = L // Q_BLOCK
    qb = q.reshape(B, nb, Q_BLOCK, H, Dk).transpose(1, 0, 2, 3, 4)

    def one_block(qi):
        s = jnp.einsum('bqhd,bkhd->bhqk', qi, k, preferred_element_type=jnp.float32) * scale
        p = jax.nn.softmax(s, axis=-1)
        return jnp.einsum('bhqk,bkhd->bqhd', p.astype(v.dtype), v)

    o = lax.map(one_block, qb)
    return o.transpose(1, 0, 2, 3, 4).reshape(B, L, H, v.shape[-1])


def mla_mixer(h, w_in, q_a_g, kv_a_g, w_q_b, w_kv_b, q_n_g, k_n_g, w_o):
    B, L, _ = h.shape
    proj = h @ w_in
    q_lat, kv_lat, k_rope = jnp.split(proj, [Q_LORA_RANK, Q_LORA_RANK + KV_LORA_RANK], axis=-1)
    q = (rms_norm(q_lat, q_a_g) @ w_q_b).reshape(B, L, MLA_HEADS, QK_HEAD_DIM)
    kv = (rms_norm(kv_lat, kv_a_g) @ w_kv_b).reshape(B, L, MLA_HEADS, QK_NOPE_DIM + V_HEAD_DIM)
    k_nope, v = jnp.split(kv, [QK_NOPE_DIM], axis=-1)
    k_rope = jnp.broadcast_to(k_rope[:, :, None, :], (B, L, MLA_HEADS, QK_ROPE_DIM))
    k = jnp.concatenate([k_nope, k_rope], axis=-1)
    q = rms_norm(q, q_n_g)
    k = rms_norm(k, k_n_g)
    cos, sin = rope_cos_sin(L, QK_ROPE_DIM)
    q = jnp.concatenate([q[..., :QK_NOPE_DIM], apply_rope(q[..., QK_NOPE_DIM:], cos, sin)], axis=-1)
    k = jnp.concatenate([k[..., :QK_NOPE_DIM], apply_rope(k[..., QK_NOPE_DIM:], cos, sin)], axis=-1)
    o = dense_bidirectional_attention(q, k, v, QK_HEAD_DIM ** -0.5)
    return o.reshape(B, L, MLA_HEADS * V_HEAD_DIM) @ w_o


def chunk_gated_scan(q, k, v, log_f):
    B, L, H, K = q.shape
    V = v.shape[-1]
    C = HGRN_CHUNK
    nc = L // C

    def to_chunks(a):
        return a.reshape(B, nc, C, H, a.shape[-1]).transpose(1, 0, 3, 2, 4)

    mask = jnp.tril(jnp.ones((C, C), dtype=bool))[None, None, :, :, None]

    def step(S, inp):
        qi, ki, vi, gi = inp
        b = jnp.cumsum(gi, axis=2)
        diff = b[:, :, :, None, :] - b[:, :, None, :, :]
        decay = jnp.exp(jnp.where(mask, diff, -jnp.inf))
        att = jnp.einsum('bhtk,bhsk,bhtsk->bhts', qi, ki, decay)
        o_intra = jnp.einsum('bhts,bhsv->bhtv', att, vi)
        o_inter = jnp.einsum('bhtk,bhkv->bhtv', qi * jnp.exp(b), S)
        b_last = b[:, :, -1:, :]
        S_new = jnp.exp(b_last)[:, :, 0, :, None] * S + jnp.einsum('bhsk,bhsv->bhkv', ki * jnp.exp(b_last - b), vi)
        return S_new, o_intra + o_inter

    S0 = jnp.zeros((B, H, K, V), jnp.float32)
    _, o = lax.scan(step, S0, (to_chunks(q), to_chunks(k), to_chunks(v), to_chunks(log_f)))
    return o.transpose(1, 0, 3, 2, 4).reshape(B, L, H, V)


def forget_gate(f_logits, lb):
    z = f_logits.astype(jnp.float32)
    log_f = jnp.logaddexp(jnp.log(lb), jnp.log1p(-lb) + jax.nn.log_sigmoid(z))
    k = (1.0 - lb) * jax.nn.sigmoid(-z)
    return log_f, k


def hgrn2_mixer(h, w_in, lb_fwd, lb_bwd, o_norm_g, w_o):
    B, L, _ = h.shape
    kw = HGRN_KEY_WIDTH
    q, f_fw, f_bw, i_in, g = jnp.split(h @ w_in, [kw, 2 * kw, 3 * kw, 3 * kw + D_MODEL], axis=-1)

    def heads(a):
        return a.reshape(B, L, HGRN_HEADS, -1)

    def flip(a):
        return jnp.flip(a, axis=1)

    q = heads(q.astype(jnp.float32))
    v = heads(i_in.astype(jnp.float32))
    lf_f, k_f = forget_gate(f_fw, lb_fwd)
    lf_b, k_b = forget_gate(f_bw, lb_bwd)
    o_fwd = chunk_gated_scan(q, heads(k_f), v, heads(lf_f))
    o_bwd = flip(chunk_gated_scan(flip(q), flip(heads(k_b)), flip(v), flip(heads(lf_b))))
    o = rms_norm(o_fwd + o_bwd, o_norm_g) * jax.nn.silu(heads(g.astype(jnp.float32)))
    return o.reshape(B, L, D_MODEL).astype(h.dtype) @ w_o


def fourier_mixer(h, w_o):
    B, L, _ = h.shape
    hg = h.astype(jnp.float32).reshape(B, L, FNET_GROUPS, FNET_GROUP_DIM)
    mixed = jnp.fft.fftn(hg, axes=(1, 3), norm='ortho').real
    return mixed.reshape(B, L, D_MODEL).astype(h.dtype) @ w_o


def memory_cross_attention(h, m, w_q, w_kv, q_g, k_g, w_o):
    B, L, _ = h.shape
    M = m.shape[1]
    q = rms_norm((h @ w_q).reshape(B, L, MEM_HEADS, MEM_HEAD_DIM), q_g)
    k, v = jnp.split(m @ w_kv, 2, axis=-1)
    k = rms_norm(k.reshape(B, M, MEM_HEADS, MEM_HEAD_DIM), k_g)
    v = v.reshape(B, M, MEM_HEADS, MEM_HEAD_DIM)
    s = jnp.einsum('bqhd,bkhd->bhqk', q, k, preferred_element_type=jnp.float32) * (MEM_HEAD_DIM ** -0.5)
    p = jax.nn.softmax(s, axis=-1)
    o = jnp.einsum('bhqk,bkhd->bqhd', p.astype(v.dtype), v)
    return o.reshape(B, L, MEM_HEADS * MEM_HEAD_DIM) @ w_o


def squared_relu_mlp(h, w1, w2):
    return jnp.square(jax.nn.relu(h @ w1)) @ w2


def run_trunk(x, mem, p):
    lb_all = jnp.cumsum(jax.nn.softmax(p['hgrn_lb_logits'].astype(jnp.float32), axis=0), axis=0)
    lb_all = lb_all - lb_all[:1]
    for i in range(DEPTH):
        kind = i % N_MIXERS
        j = i // N_MIXERS
        h = rms_norm(x, p['norm_mix_g'][i])
        if kind == 0:
            y = mla_mixer(h, p['mla_w_in'][j], p['mla_q_a_norm_g'][j], p['mla_kv_a_norm_g'][j],
                          p['mla_w_q_b'][j], p['mla_w_kv_b'][j], p['mla_q_norm_g'][j],
                          p['mla_k_norm_g'][j], p['mla_w_o'][j])
        elif kind == 1:
            y = hgrn2_mixer(h, p['hgrn_w_in'][j], lb_all[i, 0], lb_all[i, 1],
                            p['hgrn_o_norm_g'][j], p['hgrn_w_o'][j])
        else:
            y = fourier_mixer(h, p['fnet_w_o'][j])
        x = x + y
        x = x + memory_cross_attention(rms_norm(x, p['norm_xq_g'][i]), rms_norm(mem, p['norm_mem_g'][i]),
                                       p['mem_w_q'][i], p['mem_w_kv'][i], p['mem_q_norm_g'][i],
                                       p['mem_k_norm_g'][i], p['mem_w_o'][i])
        x = x + squared_relu_mlp(rms_norm(x, p['norm_mlp_g'][i]), p['mlp_w1'][i], p['mlp_w2'][i])
    return x


def setup_inputs(seed: int = 0) -> dict:
    key = jax.random.key(seed)
    ks = iter(jax.random.split(key, 48))

    def w(shape):
        return jax.random.normal(next(ks), shape, jnp.float32) * (shape[-2] ** -0.5)

    def gain(shape):
        return 1.0 + 0.02 * jax.random.normal(next(ks), shape, jnp.float32)

    def act(shape):
        return jax.random.normal(next(ks), shape, jnp.float32)

    na, nb, nf = N_MLA_LAYERS, N_HGRN_LAYERS, N_FNET_LAYERS
    return {
        'x_prompt': act((BATCH, SEQ, D_MODEL)),
        'x_sample': act((DEC_BATCH, DEC_SEQ, D_MODEL)),
        'mem_prompt': act((BATCH, MEM_TOKENS, D_MODEL)),
        'mem_sample': act((DEC_BATCH, MEM_TOKENS, D_MODEL)),
        'norm_mix_g': gain((DEPTH, D_MODEL)),
        'mla_w_in': w((na, D_MODEL, Q_LORA_RANK + KV_LORA_RANK + QK_ROPE_DIM)),
        'mla_q_a_norm_g': gain((na, Q_LORA_RANK)),
        'mla_kv_a_norm_g': gain((na, KV_LORA_RANK)),
        'mla_w_q_b': w((na, Q_LORA_RANK, MLA_HEADS * QK_HEAD_DIM)),
        'mla_w_kv_b': w((na, KV_LORA_RANK, MLA_HEADS * (QK_NOPE_DIM + V_HEAD_DIM))),
        'mla_q_norm_g': gain((na, QK_HEAD_DIM)),
        'mla_k_norm_g': gain((na, QK_HEAD_DIM)),
        'mla_w_o': w((na, MLA_HEADS * V_HEAD_DIM, D_MODEL)),
        'hgrn_w_in': w((nb, D_MODEL, 3 * HGRN_KEY_WIDTH + 2 * D_MODEL)),
        'hgrn_lb_logits': jax.random.normal(next(ks), (DEPTH, 2, HGRN_KEY_WIDTH), jnp.float32),
        'hgrn_o_norm_g': gain((nb, HGRN_VAL_DIM)),
        'hgrn_w_o': w((nb, D_MODEL, D_MODEL)),
        'fnet_w_o': w((nf, D_MODEL, D_MODEL)),
        'norm_xq_g': gain((DEPTH, D_MODEL)),
        'norm_mem_g': gain((DEPTH, D_MODEL)),
        'mem_w_q': w((DEPTH, D_MODEL, MEM_HEADS * MEM_HEAD_DIM)),
        'mem_w_kv': w((DEPTH, D_MODEL, 2 * MEM_HEADS * MEM_HEAD_DIM)),
        'mem_q_norm_g': gain((DEPTH, MEM_HEAD_DIM)),
        'mem_k_norm_g': gain((DEPTH, MEM_HEAD_DIM)),
        'mem_w_o': w((DEPTH, MEM_HEADS * MEM_HEAD_DIM, D_MODEL)),
        'norm_mlp_g': gain((DEPTH, D_MODEL)),
        'mlp_w1': w((DEPTH, D_MODEL, D_FF)),
        'mlp_w2': w((DEPTH, D_FF, D_MODEL)),
    }


def reference(x_prompt, x_sample, mem_prompt, mem_sample, norm_mix_g, mla_w_in, mla_q_a_norm_g,
              mla_kv_a_norm_g, mla_w_q_b, mla_w_kv_b, mla_q_norm_g, mla_k_norm_g, mla_w_o,
              hgrn_w_in, hgrn_lb_logits, hgrn_o_norm_g, hgrn_w_o, fnet_w_o, norm_xq_g, norm_mem_g,
              mem_w_q, mem_w_kv, mem_q_norm_g, mem_k_norm_g, mem_w_o, norm_mlp_g, mlp_w1, mlp_w2):
    p = dict(norm_mix_g=norm_mix_g, mla_w_in=mla_w_in, mla_q_a_norm_g=mla_q_a_norm_g,
             mla_kv_a_norm_g=mla_kv_a_norm_g, mla_w_q_b=mla_w_q_b, mla_w_kv_b=mla_w_kv_b,
             mla_q_norm_g=mla_q_norm_g, mla_k_norm_g=mla_k_norm_g, mla_w_o=mla_w_o,
             hgrn_w_in=hgrn_w_in, hgrn_lb_logits=hgrn_lb_logits, hgrn_o_norm_g=hgrn_o_norm_g,
             hgrn_w_o=hgrn_w_o, fnet_w_o=fnet_w_o, norm_xq_g=norm_xq_g, norm_mem_g=norm_mem_g,
             mem_w_q=mem_w_q, mem_w_kv=mem_w_kv, mem_q_norm_g=mem_q_norm_g,
             mem_k_norm_g=mem_k_norm_g, mem_w_o=mem_w_o, norm_mlp_g=norm_mlp_g,
             mlp_w1=mlp_w1, mlp_w2=mlp_w2)
    y_prompt = run_trunk(x_prompt, mem_prompt, p)
    y_sample = run_trunk(x_sample, mem_sample, p)
    return (y_prompt, y_sample)
```

```python
import functools

import jax
import jax.numpy as jnp
from jax import lax
from jax.experimental import pallas as pl
from jax.experimental.pallas import tpu as pltpu

F32 = jnp.float32
BF16 = jnp.bfloat16

EPS = 1e-6
D_MODEL = 2048
D_FF = 4 * D_MODEL

MLA_HEADS = 16
Q_LORA_RANK = 512
KV_LORA_RANK = 512
QK_NOPE_DIM = 128
QK_ROPE_DIM = 64
QK_HEAD_DIM = QK_NOPE_DIM + QK_ROPE_DIM
V_HEAD_DIM = 128
ROPE_THETA = 10000.0
QK_PAD_DIM = 256

HGRN_HEADS = 16
HGRN_KEY_DIM = 128
HGRN_VAL_DIM = D_MODEL // HGRN_HEADS
HGRN_KEY_WIDTH = HGRN_HEADS * HGRN_KEY_DIM
HGRN_CHUNK = 64

FNET_GROUPS = 4
FNET_GROUP_DIM = D_MODEL // FNET_GROUPS

MEM_HEADS = 4
MEM_HEAD_DIM = 128
MEM_WIDTH = MEM_HEADS * MEM_HEAD_DIM

LANE = 128
VMEM_LIMIT_BYTES = 56 * 1024 * 1024


def _params(*semantics):
    return pltpu.CompilerParams(dimension_semantics=semantics, vmem_limit_bytes=VMEM_LIMIT_BYTES)


def _resident(shape):
    nd = len(shape)
    return pl.BlockSpec(shape, lambda *_: (0,) * nd, pipeline_mode=pl.Buffered(1))


def _rms(x, g):
    ms = jnp.mean(x * x, axis=-1, keepdims=True)
    return x * lax.rsqrt(ms + EPS) * g


def _dot(a, b):
    return jnp.dot(a, b, preferred_element_type=F32)


def _dot_nt(a, b):
    return lax.dot_general(a, b, (((1,), (1,)), ((), ())), preferred_element_type=F32)


def _norm_matmul_kernel(x_ref, g_ref, w_ref, o_ref, xn_ref):
    @pl.when(pl.program_id(1) == 0)
    def _():
        xn_ref[...] = _rms(x_ref[...], g_ref[...]).astype(BF16)

    o_ref[...] = _dot(xn_ref[...], w_ref[...]).astype(o_ref.dtype)


def norm_matmul(x, g, w, out_dtype, tm, tn):
    t, d = x.shape
    n = w.shape[1]
    return pl.pallas_call(
        _norm_matmul_kernel,
        grid=(t // tm, n // tn),
        in_specs=[
            pl.BlockSpec((tm, d), lambda i, j: (i, 0)),
            _resident((1, d)),
            pl.BlockSpec((d, tn), lambda i, j: (0, j)),
        ],
        out_specs=pl.BlockSpec((tm, tn), lambda i, j: (i, j)),
        out_shape=jax.ShapeDtypeStruct((t, n), out_dtype),
        scratch_shapes=[pltpu.VMEM((tm, d), BF16)],
        compiler_params=_params("parallel", "arbitrary"),
        name="norm_matmul",
    )(x, g, w)


def _matmul_residual_kernel(a_ref, w_ref, r_ref, o_ref):
    o_ref[...] = r_ref[...] + _dot(a_ref[...], w_ref[...])


def matmul_residual(a, w, res, tm):
    t, k = a.shape
    n = w.shape[1]
    return pl.pallas_call(
        _matmul_residual_kernel,
        grid=(t // tm,),
        in_specs=[
            pl.BlockSpec((tm, k), lambda i: (i, 0)),
            _resident((k, n)),
            pl.BlockSpec((tm, n), lambda i: (i, 0)),
        ],
        out_specs=pl.BlockSpec((tm, n), lambda i: (i, 0)),
        out_shape=jax.ShapeDtypeStruct((t, n), F32),
        compiler_params=_params("parallel"),
        name="matmul_residual",
    )(a, w, res)


def _mlp_kernel(x_ref, g_ref, w1_ref, w2_ref, o_ref, xn_ref, acc_ref):
    j = pl.program_id(1)

    @pl.when(j == 0)
    def _():
        xn_ref[...] = _rms(x_ref[...], g_ref[...]).astype(BF16)
        acc_ref[...] = jnp.zeros_like(acc_ref)

    h = jnp.maximum(_dot(xn_ref[...], w1_ref[...]), 0.0)
    acc_ref[...] += _dot((h * h).astype(BF16), w2_ref[...])

    @pl.when(j == pl.num_programs(1) - 1)
    def _():
        o_ref[...] = x_ref[...] + acc_ref[...]


def mlp(x, g, w1, w2, tm, tf):
    t, d = x.shape
    f = w1.shape[1]
    return pl.pallas_call(
        _mlp_kernel,
        grid=(t // tm, f // tf),
        in_specs=[
            pl.BlockSpec((tm, d), lambda i, j: (i, 0)),
            _resident((1, d)),
            pl.BlockSpec((d, tf), lambda i, j: (0, j)),
            pl.BlockSpec((tf, d), lambda i, j: (j, 0)),
        ],
        out_specs=pl.BlockSpec((tm, d), lambda i, j: (i, 0)),
        out_shape=jax.ShapeDtypeStruct((t, d), F32),
        scratch_shapes=[pltpu.VMEM((tm, d), BF16), pltpu.VMEM((tm, d), F32)],
        compiler_params=_params("parallel", "arbitrary"),
        name="mlp",
    )(x, g, w1, w2)


def _mem_kv_kernel(m_ref, g_ref, w_ref, kg_ref, k_out, v_out):
    mn = _rms(m_ref[...], g_ref[...]).astype(BF16)
    kv = _dot(mn, w_ref[...])
    kg = kg_ref[...]
    for h in range(MEM_HEADS):
        lo = h * MEM_HEAD_DIM
        k_out[:, lo:lo + MEM_HEAD_DIM] = _rms(kv[:, lo:lo + MEM_HEAD_DIM], kg).astype(BF16)
    v_out[...] = kv[:, MEM_WIDTH:].astype(BF16)


def mem_kv(mem, g, w_kv, k_g):
    b, m, d = mem.shape
    spec_out = pl.BlockSpec((None, m, MEM_WIDTH), lambda i: (i, 0, 0))
    return pl.pallas_call(
        _mem_kv_kernel,
        grid=(b,),
        in_specs=[
            pl.BlockSpec((None, m, d), lambda i: (i, 0, 0)),
            _resident((1, d)),
            _resident((d, 2 * MEM_WIDTH)),
            _resident((1, MEM_HEAD_DIM)),
        ],
        out_specs=[spec_out, spec_out],
        out_shape=[jax.ShapeDtypeStruct((b, m, MEM_WIDTH), BF16)] * 2,
        compiler_params=_params("parallel"),
        name="mem_kv",
    )(mem, g, w_kv, k_g)


def _mem_xattn_kernel(x_ref, g_ref, wq_ref, qg_ref, k_ref, v_ref, wo_ref, o_ref):
    x = x_ref[...]
    q = _dot(_rms(x, g_ref[...]).astype(BF16), wq_ref[...])
    qg = qg_ref[...]
    heads = []
    for h in range(MEM_HEADS):
        lo = h * MEM_HEAD_DIM
        qh = _rms(q[:, lo:lo + MEM_HEAD_DIM], qg).astype(BF16)
        s = _dot_nt(qh, k_ref[:, lo:lo + MEM_HEAD_DIM])
        e = jnp.exp(s - jnp.max(s, axis=-1, keepdims=True))
        p = e / jnp.sum(e, axis=-1, keepdims=True)
        heads.append(_dot(p.astype(BF16), v_ref[:, lo:lo + MEM_HEAD_DIM]))
    o = jnp.concatenate(heads, axis=-1).astype(BF16)
    o_ref[...] = x + _dot(o, wo_ref[...])


def mem_xattn(x, g, w_q, q_g_scaled, k, v, w_o, tm):
    b, l, d = x.shape
    m = k.shape[1]
    spec_x = pl.BlockSpec((None, tm, d), lambda i, j: (i, j, 0))
    spec_kv = pl.BlockSpec((None, m, MEM_WIDTH), lambda i, j: (i, 0, 0))
    return pl.pallas_call(
        _mem_xattn_kernel,
        grid=(b, l // tm),
        in_specs=[
            spec_x,
            _resident((1, d)),
            _resident((d, MEM_WIDTH)),
            _resident((1, MEM_HEAD_DIM)),
            spec_kv,
            spec_kv,
            _resident((MEM_WIDTH, d)),
        ],
        out_specs=spec_x,
        out_shape=jax.ShapeDtypeStruct((b, l, d), F32),
        compiler_params=_params("parallel", "parallel"),
        name="mem_xattn",
    )(x, g, w_q, q_g_scaled, k, v, w_o)


def _mla_pre_kernel(x_ref, gmix_ref, win_ref, qag_ref, kvag_ref, wqb_ref, wkvb_ref, gq_ref, gk_ref,
                    cos_ref, sin_lo_ref, sin_hi_ref, q_out, k_out, v_out):
    xn = _rms(x_ref[...], gmix_ref[...]).astype(BF16)
    proj = _dot(xn, win_ref[...])
    ql = _rms(proj[:, :Q_LORA_RANK], qag_ref[...]).astype(BF16)
    kvl = _rms(proj[:, Q_LORA_RANK:Q_LORA_RANK + KV_LORA_RANK], kvag_ref[...]).astype(BF16)
    kr = proj[:, Q_LORA_RANK + KV_LORA_RANK:]
    qf = _dot(ql, wqb_ref[...])
    kvf = _dot(kvl, wkvb_ref[...])
    cos = cos_ref[...]
    sin_lo = sin_lo_ref[...]
    sin_hi = sin_hi_ref[...]

    def rope(r):
        half = QK_ROPE_DIM // 2
        return r * cos + pltpu.roll(r, half, 1) * sin_hi + pltpu.roll(r, LANE - half, 1) * sin_lo

    gq = gq_ref[...]
    gk = gk_ref[...]
    inv_dim = 1.0 / QK_HEAD_DIM
    ss_kr = jnp.sum(kr * kr, axis=-1, keepdims=True)
    kr_rot = rope(kr * gk[:, QK_NOPE_DIM:])
    for h in range(MLA_HEADS):
        lo = h * QK_PAD_DIM
        qh = qf[:, lo:lo + QK_PAD_DIM]
        inv_q = lax.rsqrt(jnp.sum(qh * qh, axis=-1, keepdims=True) * inv_dim + EPS)
        qn = qh * inv_q * gq
        q_out[h, :, :QK_NOPE_DIM] = qn[:, :QK_NOPE_DIM].astype(BF16)
        q_out[h, :, QK_NOPE_DIM:] = rope(qn[:, QK_NOPE_DIM:]).astype(BF16)
        kn = kvf[:, lo:lo + QK_NOPE_DIM]
        inv_k = lax.rsqrt((jnp.sum(kn * kn, axis=-1, keepdims=True) + ss_kr) * inv_dim + EPS)
        k_out[h, :, :QK_NOPE_DIM] = (kn * inv_k * gk[:, :QK_NOPE_DIM]).astype(BF16)
        k_out[h, :, QK_NOPE_DIM:] = (kr_rot * inv_k).astype(BF16)
        v_out[h] = kvf[:, lo + QK_NOPE_DIM:lo + QK_PAD_DIM].astype(BF16)


def mla_pre(x, gmix, w_in_p, qag, kvag, w_qb_p, w_kvb, gq_p, gk_p, cos_t, sin_lo_t, sin_hi_t, tm):
    b, l, d = x.shape
    h = MLA_HEADS
    spec_tab = pl.BlockSpec((tm, LANE), lambda i, j: (j, 0))
    spec_qk = pl.BlockSpec((None, h, tm, QK_PAD_DIM), lambda i, j: (i, 0, j, 0))
    spec_v = pl.BlockSpec((None, h, tm, V_HEAD_DIM), lambda i, j: (i, 0, j, 0))
    return pl.pallas_call(
        _mla_pre_kernel,
        grid=(b, l // tm),
        in_specs=[
            pl.BlockSpec((None, tm, d), lambda i, j: (i, j, 0)),
            _resident((1, d)),
            _resident(w_in_p.shape),
            _resident((1, Q_LORA_RANK)),
            _resident((1, KV_LORA_RANK)),
            _resident(w_qb_p.shape),
            _resident(w_kvb.shape),
            _resident((1, QK_PAD_DIM)),
            _resident((1, QK_PAD_DIM)),
            spec_tab, spec_tab, spec_tab,
        ],
        out_specs=[spec_qk, spec_qk, spec_v],
        out_shape=[
            jax.ShapeDtypeStruct((b, h, l, QK_PAD_DIM), BF16),
            jax.ShapeDtypeStruct((b, h, l, QK_PAD_DIM), BF16),
            jax.ShapeDtypeStruct((b, h, l, V_HEAD_DIM), BF16),
        ],
        compiler_params=_params("parallel", "parallel"),
        name="mla_pre",
    )(x, gmix, w_in_p, qag, kvag, w_qb_p, w_kvb, gq_p, gk_p, cos_t, sin_lo_t, sin_hi_t)


def _flash_kernel(q_ref, k_ref, v_ref, o_ref, m_ref, l_ref, acc_ref):
    kv = pl.program_id(3)

    @pl.when(kv == 0)
    def _():
        m_ref[...] = jnp.full_like(m_ref, -jnp.inf)
        l_ref[...] = jnp.zeros_like(l_ref)
        acc_ref[...] = jnp.zeros_like(acc_ref)

    s = _dot_nt(q_ref[...], k_ref[...])
    m_prev = m_ref[...]
    m_new = jnp.maximum(m_prev, jnp.max(s, axis=-1, keepdims=True))
    alpha = jnp.exp(m_prev - m_new)
    p = jnp.exp(s - m_new)
    l_ref[...] = alpha * l_ref[...] + jnp.sum(p, axis=-1, keepdims=True)
    acc_ref[...] = alpha * acc_ref[...] + _dot(p.astype(BF16), v_ref[...])
    m_ref[...] = m_new

    @pl.when(kv == pl.num_programs(3) - 1)
    def _():
        o_ref[...] = (acc_ref[...] / l_ref[...]).astype(o_ref.dtype)


def flash_attention(q, k, v, tq, tk):
    b, h, l, dk = q.shape
    dv = v.shape[-1]
    return pl.pallas_call(
        _flash_kernel,
        grid=(b, h, l // tq, l // tk),
        in_specs=[
            pl.BlockSpec((None, None, tq, dk), lambda i, j, a, c: (i, j, a, 0)),
            pl.BlockSpec((None, None, tk, dk), lambda i, j, a, c: (i, j, c, 0)),
            pl.BlockSpec((None, None, tk, dv), lambda i, j, a, c: (i, j, c, 0)),
        ],
        out_specs=pl.BlockSpec((None, tq, dv), lambda i, j, a, c: (i, a, j)),
        out_shape=jax.ShapeDtypeStruct((b, l, h * dv), BF16),
        scratch_shapes=[pltpu.VMEM((tq, 1), F32), pltpu.VMEM((tq, 1), F32), pltpu.VMEM((tq, dv), F32)],
        compiler_params=_params("parallel", "parallel", "parallel", "arbitrary"),
        name="flash_attention",
    )(q, k, v)


def _hgrn_scan_kernel(*refs, reverse, n_chunks):
    if reverse:
        q_ref, z_ref, v_ref, lb_ref, of_ref, gate_ref, ong_ref, o_ref, s_ref, b_ref, kk_ref, att_ref = refs
    else:
        q_ref, z_ref, v_ref, lb_ref, o_ref, s_ref, b_ref, kk_ref, att_ref = refs
    c = HGRN_CHUNK
    kd = HGRN_KEY_DIM

    @pl.when(pl.program_id(2) == 0)
    def _():
        s_ref[...] = jnp.zeros_like(s_ref)

    lb = lb_ref[...]
    z = z_ref[...]
    log_sig = jnp.minimum(z, 0.0) - jnp.log1p(jnp.exp(-jnp.abs(z)))
    a = jnp.log(lb)
    cterm = jnp.log1p(-lb) + log_sig
    mx = jnp.maximum(a, cterm)
    log_f = mx + jnp.log(jnp.exp(a - mx) + jnp.exp(cterm - mx))
    kk = (1.0 - lb) * jax.nn.sigmoid(-z)

    row = lax.broadcasted_iota(jnp.int32, (c, c), 0)
    col = lax.broadcasted_iota(jnp.int32, (c, c), 1)
    tri = (col >= row) if reverse else (col <= row)
    tri_f = tri.astype(F32)
    for ci in range(n_chunks):
        b_ref[ci] = jnp.dot(tri_f, log_f[ci * c:(ci + 1) * c], precision=lax.Precision.HIGHEST,
                            preferred_element_type=F32)
    kk_ref[...] = kk.reshape(n_chunks, c, kd)

    q3 = q_ref[...].reshape(n_chunks, c, kd)
    b3 = b_ref[...]
    t_idx = lax.broadcasted_iota(jnp.int32, (1, c, 1), 1)
    sel_row = lax.broadcasted_iota(jnp.int32, (kd, c), 1)
    att_ref[...] = jnp.zeros_like(att_ref)

    def col_step(s, carry):
        k_row = kk_ref[:, pl.ds(s, 1), :]
        b_row = b_ref[:, pl.ds(s, 1), :]
        w = q3 * k_row * jnp.exp(jnp.minimum(b3 - b_row, 0.0))
        visible = (t_idx <= s) if reverse else (t_idx >= s)
        w = jnp.where(visible, w, 0.0).reshape(n_chunks * c, kd).astype(BF16)
        onehot = (sel_row == s).astype(BF16)
        att_ref[...] += _dot(w, onehot)
        return carry

    lax.fori_loop(0, c, col_step, 0)

    order = range(n_chunks - 1, -1, -1) if reverse else range(n_chunks)
    edge = 0 if reverse else c - 1
    for ci in order:
        rows = slice(ci * c, (ci + 1) * c)
        bc = b_ref[ci]
        qc = q_ref[rows, :]
        kc = kk_ref[ci]
        vc = v_ref[rows, :]
        b_edge = bc[edge:edge + 1, :]
        st = s_ref[...]
        o_intra = _dot(att_ref[rows, :].astype(BF16), vc.astype(BF16))
        o_inter = _dot_nt((qc * jnp.exp(bc)).astype(BF16), st.astype(BF16))
        k_dec = (kc * jnp.exp(b_edge - bc)).astype(BF16)
        s_ref[...] = st * jnp.exp(b_edge) + _dot(vc.T.astype(BF16), k_dec)
        o = o_intra + o_inter
        if reverse:
            o = _rms(o + of_ref[rows, :], ong_ref[...])
            g = gate_ref[rows, :]
            o_ref[rows, :] = (o * (g * jax.nn.sigmoid(g))).astype(o_ref.dtype)
        else:
            o_ref[rows, :] = o


def hgrn_scan(proj, lb, reverse, tl, o_fwd=None, o_norm_g=None):
    b, l, _ = proj.shape
    h = HGRN_HEADS
    nl = l // tl
    n_chunks = tl // HGRN_CHUNK
    kd = HGRN_KEY_DIM

    def pos(j):
        return (nl - 1 - j) if reverse else j

    def col_spec(section):
        return pl.BlockSpec((None, tl, kd), lambda i, hh, j: (i, pos(j), section * h + hh))

    in_specs = [col_spec(0), col_spec(2 if reverse else 1), col_spec(3),
                pl.BlockSpec((1, kd), lambda i, hh, j: (0, hh))]
    args = [proj, proj, proj, lb]
    out_spec = pl.BlockSpec((None, tl, kd), lambda i, hh, j: (i, pos(j), hh))
    if reverse:
        in_specs += [out_spec, col_spec(4), _resident((1, HGRN_VAL_DIM))]
        args += [o_fwd, proj, o_norm_g]
    return pl.pallas_call(
        functools.partial(_hgrn_scan_kernel, reverse=reverse, n_chunks=n_chunks),
        grid=(b, h, nl),
        in_specs=in_specs,
        out_specs=out_spec,
        out_shape=jax.ShapeDtypeStruct((b, l, D_MODEL), BF16 if reverse else F32),
        scratch_shapes=[
            pltpu.VMEM((HGRN_VAL_DIM, kd), F32),
            pltpu.VMEM((n_chunks, HGRN_CHUNK, kd), F32),
            pltpu.VMEM((n_chunks, HGRN_CHUNK, kd), F32),
            pltpu.VMEM((tl, HGRN_CHUNK), F32),
        ],
        compiler_params=_params("parallel", "parallel", "arbitrary"),
        name="hgrn_scan_bwd" if reverse else "hgrn_scan_fwd",
    )(*args)


def _fnet_channel_kernel(x_ref, g_ref, cn_ref, sn_ref, u_ref, w_ref):
    xn = _rms(x_ref[...], g_ref[...]).astype(BF16)
    for gi in range(FNET_GROUPS):
        cols = slice(gi * FNET_GROUP_DIM, (gi + 1) * FNET_GROUP_DIM)
        u_ref[:, cols] = _dot(xn[:, cols], cn_ref[...]).astype(BF16)
        w_ref[:, cols] = _dot(xn[:, cols], sn_ref[...]).astype(BF16)


def fnet_channel(x, g, cn, sn, tm):
    t, d = x.shape
    spec = pl.BlockSpec((tm, d), lambda i: (i, 0))
    return pl.pallas_call(
        _fnet_channel_kernel,
        grid=(t // tm,),
        in_specs=[spec, _resident((1, d)), _resident(cn.shape), _resident(sn.shape)],
        out_specs=[spec, spec],
        out_shape=[jax.ShapeDtypeStruct((t, d), BF16)] * 2,
        compiler_params=_params("parallel"),
        name="fnet_channel",
    )(x, g, cn, sn)


def _fnet_seq_kernel(cl_ref, nsl_ref, u_ref, w_ref, x_ref, wo_ref, o_ref, acc_ref):
    k = pl.program_id(2)

    @pl.when(k == 0)
    def _():
        acc_ref[...] = jnp.zeros_like(acc_ref)

    acc_ref[...] += _dot(cl_ref[...], u_ref[...]) + _dot(nsl_ref[...], w_ref[...])

    @pl.when(k == pl.num_programs(2) - 1)
    def _():
        o_ref[...] = x_ref[...] + _dot(acc_ref[...].astype(BF16), wo_ref[...])


def fnet_seq(cl, nsl, u, w, x, w_o, tm, tk):
    b, l, d = x.shape
    spec_dft = pl.BlockSpec((tm, tk), lambda i, j, k: (j, k))
    spec_uw = pl.BlockSpec((None, tk, d), lambda i, j, k: (i, k, 0))
    spec_x = pl.BlockSpec((None, tm, d), lambda i, j, k: (i, j, 0))
    return pl.pallas_call(
        _fnet_seq_kernel,
        grid=(b, l // tm, l // tk),
        in_specs=[spec_dft, spec_dft, spec_uw, spec_uw, spec_x, _resident(w_o.shape)],
        out_specs=spec_x,
        out_shape=jax.ShapeDtypeStruct((b, l, d), F32),
        scratch_shapes=[pltpu.VMEM((tm, d), F32)],
        compiler_params=_params("parallel", "parallel", "arbitrary"),
        name="fnet_seq",
    )(cl, nsl, u, w, x, w_o)


def _dft_tables(n):
    idx = jnp.arange(n, dtype=jnp.int32)
    jk = (idx[:, None] * idx[None, :]) % n
    ang = jk.astype(F32) * (2.0 * jnp.pi / n)
    scale = n ** -0.5
    return jnp.cos(ang) * scale, jnp.sin(ang) * scale


def _rope_tables(length):
    dim = QK_ROPE_DIM
    half = dim // 2
    inv = 1.0 / (ROPE_THETA ** (jnp.arange(0, dim, 2, dtype=F32) / dim))
    ang = jnp.arange(length, dtype=F32)[:, None] * inv[None, :]
    cos, sin = jnp.cos(ang), jnp.sin(ang)
    zero = jnp.zeros_like(cos)
    cos_t = jnp.concatenate([cos, cos, zero, zero], axis=-1)
    sin_lo = jnp.concatenate([-sin, zero, zero, zero], axis=-1)
    sin_hi = jnp.concatenate([zero, sin, zero, zero], axis=-1)
    return cos_t, sin_lo, sin_hi


def _pad_cols(a, n):
    return jnp.concatenate([a, jnp.zeros(a.shape[:-1] + (n,), a.dtype)], axis=-1)


def _row(v):
    return v.reshape(1, -1).astype(F32)


def _tile(n, cap):
    return min(n, cap)


def kernel(x_prompt, x_sample, mem_prompt, mem_sample, norm_mix_g, mla_w_in, mla_q_a_norm_g, mla_kv_a_norm_g, mla_w_q_b, mla_w_kv_b, mla_q_norm_g, mla_k_norm_g, mla_w_o, hgrn_w_in, hgrn_lb_logits, hgrn_o_norm_g, hgrn_w_o, fnet_w_o, norm_xq_g, norm_mem_g, mem_w_q, mem_w_kv, mem_q_norm_g, mem_k_norm_g, mem_w_o, norm_mlp_g, mlp_w1, mlp_w2):
    depth = norm_mix_g.shape[0]
    n_mixers = 3

    lb_all = jnp.cumsum(jax.nn.softmax(hgrn_lb_logits.astype(F32), axis=0), axis=0)
    lb_all = lb_all - lb_all[:1]

    mla_w_in_p = _pad_cols(mla_w_in, LANE - QK_ROPE_DIM).astype(BF16)
    n_mla = mla_w_in.shape[0]
    wqb = mla_w_q_b.reshape(n_mla, Q_LORA_RANK, MLA_HEADS, QK_HEAD_DIM)
    mla_w_qb_p = _pad_cols(wqb, QK_PAD_DIM - QK_HEAD_DIM).reshape(n_mla, Q_LORA_RANK, MLA_HEADS * QK_PAD_DIM).astype(BF16)
    mla_w_kvb = mla_w_kv_b.astype(BF16)
    mla_gq_p = _pad_cols(mla_q_norm_g * (QK_HEAD_DIM ** -0.5), QK_PAD_DIM - QK_HEAD_DIM)
    mla_gk_p = _pad_cols(mla_k_norm_g, QK_PAD_DIM - QK_HEAD_DIM)
    mla_w_o_b = mla_w_o.astype(BF16)
    hgrn_w_in_b = hgrn_w_in.astype(BF16)
    hgrn_w_o_b = hgrn_w_o.astype(BF16)
    fnet_w_o_b = fnet_w_o.astype(BF16)
    mem_w_q_b = mem_w_q.astype(BF16)
    mem_w_kv_b = mem_w_kv.astype(BF16)
    mem_w_o_b = mem_w_o.astype(BF16)
    mem_qg_scaled = mem_q_norm_g * (MEM_HEAD_DIM ** -0.5)
    mlp_w1_b = mlp_w1.astype(BF16)
    mlp_w2_b = mlp_w2.astype(BF16)

    has_fnet = depth >= n_mixers
    if has_fnet:
        cn, sn = _dft_tables(FNET_GROUP_DIM)
        cn, sn = cn.astype(BF16), sn.astype(BF16)

    def run_trunk(x, mem):
        b, l, d = x.shape
        t = b * l
        tm = _tile(l, 512)
        if depth >= 1:
            rope_tabs = _rope_tables(l)
        if has_fnet:
            cl, sl = _dft_tables(l)
            cl, nsl = cl.astype(BF16), (-sl).astype(BF16)
        for i in range(depth):
            kind = i % n_mixers
            j = i // n_mixers
            gmix = _row(norm_mix_g[i])
            if kind == 0:
                q, k, v = mla_pre(x, gmix, mla_w_in_p[j], _row(mla_q_a_norm_g[j]), _row(mla_kv_a_norm_g[j]),
                                  mla_w_qb_p[j], mla_w_kvb[j], _row(mla_gq_p[j]), _row(mla_gk_p[j]),
                                  *rope_tabs, tm=_tile(l, 256))
                o = flash_attention(q, k, v, tq=_tile(l, 1024), tk=_tile(l, 1024))
                x = matmul_residual(o.reshape(t, d), mla_w_o_b[j], x.reshape(t, d), tm).reshape(b, l, d)
            elif kind == 1:
                proj = norm_matmul(x.reshape(t, d), gmix, hgrn_w_in_b[j], F32, tm, 2048).reshape(b, l, -1)
                tl = _tile(l, 512)
                o_f = hgrn_scan(proj, _row(lb_all[i, 0]), False, tl)
                y = hgrn_scan(proj, _row(lb_all[i, 1]), True, tl, o_fwd=o_f, o_norm_g=_row(hgrn_o_norm_g[j]))
                x = matmul_residual(y.reshape(t, d), hgrn_w_o_b[j], x.reshape(t, d), tm).reshape(b, l, d)
            else:
                u, w = fnet_channel(x.reshape(t, d), gmix, cn, sn, tm)
                x = fnet_seq(cl, nsl, u.reshape(b, l, d), w.reshape(b, l, d), x, fnet_w_o_b[j], tm, tm)
            mk, mv = mem_kv(mem, _row(norm_mem_g[i]), mem_w_kv_b[i], _row(mem_k_norm_g[i]))
            x = mem_xattn(x, _row(norm_xq_g[i]), mem_w_q_b[i], _row(mem_qg_scaled[i]), mk, mv, mem_w_o_b[i], tm)
            x = mlp(x.reshape(t, d), _row(norm_mlp_g[i]), mlp_w1_b[i], mlp_w2_b[i], tm, 512).reshape(b, l, d)
        return x

    return (run_trunk(x_prompt, mem_prompt), run_trunk(x_sample, mem_sample))
```

```python
import functools

import jax
import jax.numpy as jnp
from jax import lax
from jax.experimental import pallas as pl
from jax.experimental.pallas import tpu as pltpu

F32 = jnp.float32
BF16 = jnp.bfloat16

EPS = 1e-6
D_MODEL = 2048
D_FF = 4 * D_MODEL

MLA_HEADS = 16
Q_LORA_RANK = 512
KV_LORA_RANK = 512
QK_NOPE_DIM = 128
QK_ROPE_DIM = 64
QK_HEAD_DIM = QK_NOPE_DIM + QK_ROPE_DIM
V_HEAD_DIM = 128
ROPE_THETA = 10000.0
QK_PAD_DIM = 256
SCORE_LOG2_LIMIT = 100.0

HGRN_HEADS = 16
HGRN_KEY_DIM = 128
HGRN_VAL_DIM = D_MODEL // HGRN_HEADS
HGRN_KEY_WIDTH = HGRN_HEADS * HGRN_KEY_DIM
HGRN_CHUNK = 64
HGRN_SUB = 16
LOG2_E = 1.4426950408889634

FNET_GROUPS = 4
FNET_GROUP_DIM = D_MODEL // FNET_GROUPS

MEM_HEADS = 4
MEM_HEAD_DIM = 128
MEM_WIDTH = MEM_HEADS * MEM_HEAD_DIM

LANE = 128
VMEM_LIMIT_BYTES = 56 * 1024 * 1024


def _params(*semantics):
    return pltpu.CompilerParams(dimension_semantics=semantics, vmem_limit_bytes=VMEM_LIMIT_BYTES)


def _resident(shape):
    nd = len(shape)
    return pl.BlockSpec(shape, lambda *_: (0,) * nd, pipeline_mode=pl.Buffered(1))


def _rms(x, g):
    ms = jnp.mean(x * x, axis=-1, keepdims=True)
    return x * lax.rsqrt(ms + EPS) * g


def _dot(a, b):
    return jnp.dot(a, b, preferred_element_type=F32)


def _dot_nt(a, b):
    return lax.dot_general(a, b, (((1,), (1,)), ((), ())), preferred_element_type=F32)


def _norm_matmul_kernel(x_ref, g_ref, w_ref, o_ref, xn_ref):
    @pl.when(pl.program_id(1) == 0)
    def _():
        xn_ref[...] = _rms(x_ref[...], g_ref[...]).astype(BF16)

    o_ref[...] = _dot(xn_ref[...], w_ref[...]).astype(o_ref.dtype)


def norm_matmul(x, g, w, out_dtype, tm, tn):
    t, d = x.shape
    n = w.shape[1]
    return pl.pallas_call(
        _norm_matmul_kernel,
        grid=(t // tm, n // tn),
        in_specs=[
            pl.BlockSpec((tm, d), lambda i, j: (i, 0)),
            _resident((1, d)),
            pl.BlockSpec((d, tn), lambda i, j: (0, j)),
        ],
        out_specs=pl.BlockSpec((tm, tn), lambda i, j: (i, j)),
        out_shape=jax.ShapeDtypeStruct((t, n), out_dtype),
        scratch_shapes=[pltpu.VMEM((tm, d), BF16)],
        compiler_params=_params("parallel", "arbitrary"),
        name="norm_matmul",
    )(x, g, w)


def _matmul_residual_kernel(a_ref, w_ref, r_ref, o_ref):
    o_ref[...] = r_ref[...] + _dot(a_ref[...], w_ref[...])


def matmul_residual(a, w, res, tm):
    t, k = a.shape
    n = w.shape[1]
    return pl.pallas_call(
        _matmul_residual_kernel,
        grid=(t // tm,),
        in_specs=[
            pl.BlockSpec((tm, k), lambda i: (i, 0)),
            _resident((k, n)),
            pl.BlockSpec((tm, n), lambda i: (i, 0)),
        ],
        out_specs=pl.BlockSpec((tm, n), lambda i: (i, 0)),
        out_shape=jax.ShapeDtypeStruct((t, n), F32),
        compiler_params=_params("parallel"),
        name="matmul_residual",
    )(a, w, res)


def _mlp_kernel(x_ref, g_ref, w1_ref, w2_ref, o_ref, xn_ref, acc_ref):
    j = pl.program_id(1)

    @pl.when(j == 0)
    def _():
        xn_ref[...] = _rms(x_ref[...], g_ref[...]).astype(BF16)
        acc_ref[...] = jnp.zeros_like(acc_ref)

    h = jnp.maximum(_dot(xn_ref[...], w1_ref[...]), 0.0)
    acc_ref[...] += _dot((h * h).astype(BF16), w2_ref[...])

    @pl.when(j == pl.num_programs(1) - 1)
    def _():
        o_ref[...] = x_ref[...] + acc_ref[...]


def mlp(x, g, w1, w2, tm, tf):
    t, d = x.shape
    f = w1.shape[1]
    return pl.pallas_call(
        _mlp_kernel,
        grid=(t // tm, f // tf),
        in_specs=[
            pl.BlockSpec((tm, d), lambda i, j: (i, 0)),
            _resident((1, d)),
            pl.BlockSpec((d, tf), lambda i, j: (0, j)),
            pl.BlockSpec((tf, d), lambda i, j: (j, 0)),
        ],
        out_specs=pl.BlockSpec((tm, d), lambda i, j: (i, 0)),
        out_shape=jax.ShapeDtypeStruct((t, d), F32),
        scratch_shapes=[pltpu.VMEM((tm, d), BF16), pltpu.VMEM((tm, d), F32)],
        compiler_params=_params("parallel", "arbitrary"),
        name="mlp",
    )(x, g, w1, w2)


def _mem_kv_kernel(m_ref, g_ref, w_ref, kg_ref, k_out, v_out):
    mn = _rms(m_ref[...], g_ref[...]).astype(BF16)
    kv = _dot(mn, w_ref[...])
    kg = kg_ref[...]
    for h in range(MEM_HEADS):
        lo = h * MEM_HEAD_DIM
        k_out[:, lo:lo + MEM_HEAD_DIM] = _rms(kv[:, lo:lo + MEM_HEAD_DIM], kg).astype(BF16)
    v_out[...] = kv[:, MEM_WIDTH:].astype(BF16)


def mem_kv(mem, g, w_kv, k_g):
    b, m, d = mem.shape
    spec_out = pl.BlockSpec((None, m, MEM_WIDTH), lambda i: (i, 0, 0))
    return pl.pallas_call(
        _mem_kv_kernel,
        grid=(b,),
        in_specs=[
            pl.BlockSpec((None, m, d), lambda i: (i, 0, 0)),
            _resident((1, d)),
            _resident((d, 2 * MEM_WIDTH)),
            _resident((1, MEM_HEAD_DIM)),
        ],
        out_specs=[spec_out, spec_out],
        out_shape=[jax.ShapeDtypeStruct((b, m, MEM_WIDTH), BF16)] * 2,
        compiler_params=_params("parallel"),
        name="mem_kv",
    )(mem, g, w_kv, k_g)


def _mem_xattn_kernel(x_ref, g_ref, wq_ref, qg_ref, k_ref, v_ref, wo_ref, o_ref):
    x = x_ref[...]
    q = _dot(_rms(x, g_ref[...]).astype(BF16), wq_ref[...])
    qg = qg_ref[...]
    heads = []
    for h in range(MEM_HEADS):
        lo = h * MEM_HEAD_DIM
        qh = _rms(q[:, lo:lo + MEM_HEAD_DIM], qg).astype(BF16)
        s = _dot_nt(qh, k_ref[:, lo:lo + MEM_HEAD_DIM])
        e = jnp.exp(s - jnp.max(s, axis=-1, keepdims=True))
        p = e / jnp.sum(e, axis=-1, keepdims=True)
        heads.append(_dot(p.astype(BF16), v_ref[:, lo:lo + MEM_HEAD_DIM]))
    o = jnp.concatenate(heads, axis=-1).astype(BF16)
    o_ref[...] = x + _dot(o, wo_ref[...])


def mem_xattn(x, g, w_q, q_g_scaled, k, v, w_o, tm):
    b, l, d = x.shape
    m = k.shape[1]
    spec_x = pl.BlockSpec((None, tm, d), lambda i, j: (i, j, 0))
    spec_kv = pl.BlockSpec((None, m, MEM_WIDTH), lambda i, j: (i, 0, 0))
    return pl.pallas_call(
        _mem_xattn_kernel,
        grid=(b, l // tm),
        in_specs=[
            spec_x,
            _resident((1, d)),
            _resident((d, MEM_WIDTH)),
            _resident((1, MEM_HEAD_DIM)),
            spec_kv,
            spec_kv,
            _resident((MEM_WIDTH, d)),
        ],
        out_specs=spec_x,
        out_shape=jax.ShapeDtypeStruct((b, l, d), F32),
        compiler_params=_params("parallel", "parallel"),
        name="mem_xattn",
    )(x, g, w_q, q_g_scaled, k, v, w_o)


def _mla_pre_kernel(x_ref, gmix_ref, win_ref, qag_ref, kvag_ref, wqb_ref, wkvb_ref, gq_ref, gk_ref,
                    cos_ref, sin_lo_ref, sin_hi_ref, q_out, k_out, v_out):
    xn = _rms(x_ref[...], gmix_ref[...]).astype(BF16)
    proj = _dot(xn, win_ref[...])
    ql = _rms(proj[:, :Q_LORA_RANK], qag_ref[...]).astype(BF16)
    kvl = _rms(proj[:, Q_LORA_RANK:Q_LORA_RANK + KV_LORA_RANK], kvag_ref[...]).astype(BF16)
    kr = proj[:, Q_LORA_RANK + KV_LORA_RANK:]
    qf = _dot(ql, wqb_ref[...])
    kvf = _dot(kvl, wkvb_ref[...])
    cos = cos_ref[...]
    sin_lo = sin_lo_ref[...]
    sin_hi = sin_hi_ref[...]

    def rope(r):
        half = QK_ROPE_DIM // 2
        return r * cos + pltpu.roll(r, half, 1) * sin_hi + pltpu.roll(r, LANE - half, 1) * sin_lo

    gq = gq_ref[...]
    gk = gk_ref[...]
    inv_dim = 1.0 / QK_HEAD_DIM
    ss_kr = jnp.sum(kr * kr, axis=-1, keepdims=True)
    kr_rot = rope(kr * gk[:, QK_NOPE_DIM:])
    for h in range(MLA_HEADS):
        lo = h * QK_PAD_DIM
        qh = qf[:, lo:lo + QK_PAD_DIM]
        inv_q = lax.rsqrt(jnp.sum(qh * qh, axis=-1, keepdims=True) * inv_dim + EPS)
        qn = qh * inv_q * gq
        q_out[h, :, :QK_NOPE_DIM] = qn[:, :QK_NOPE_DIM].astype(BF16)
        q_out[h, :, QK_NOPE_DIM:] = rope(qn[:, QK_NOPE_DIM:]).astype(BF16)
        kn = kvf[:, lo:lo + QK_NOPE_DIM]
        inv_k = lax.rsqrt((jnp.sum(kn * kn, axis=-1, keepdims=True) + ss_kr) * inv_dim + EPS)
        k_out[h, :, :QK_NOPE_DIM] = (kn * inv_k * gk[:, :QK_NOPE_DIM]).astype(BF16)
        k_out[h, :, QK_NOPE_DIM:] = (kr_rot * inv_k).astype(BF16)
        v_out[h] = kvf[:, lo + QK_NOPE_DIM:lo + QK_PAD_DIM].astype(BF16)


def mla_pre(x, gmix, w_in_p, qag, kvag, w_qb_p, w_kvb, gq_p, gk_p, cos_t, sin_lo_t, sin_hi_t, tm):
    b, l, d = x.shape
    h = MLA_HEADS
    spec_tab = pl.BlockSpec((tm, LANE), lambda i, j: (j, 0))
    spec_qk = pl.BlockSpec((None, h, tm, QK_PAD_DIM), lambda i, j: (i, 0, j, 0))
    spec_v = pl.BlockSpec((None, h, tm, V_HEAD_DIM), lambda i, j: (i, 0, j, 0))
    return pl.pallas_call(
        _mla_pre_kernel,
        grid=(b, l // tm),
        in_specs=[
            pl.BlockSpec((None, tm, d), lambda i, j: (i, j, 0)),
            _resident((1, d)),
            _resident(w_in_p.shape),
            _resident((1, Q_LORA_RANK)),
            _resident((1, KV_LORA_RANK)),
            _resident(w_qb_p.shape),
            _resident(w_kvb.shape),
            _resident((1, QK_PAD_DIM)),
            _resident((1, QK_PAD_DIM)),
            spec_tab, spec_tab, spec_tab,
        ],
        out_specs=[spec_qk, spec_qk, spec_v],
        out_shape=[
            jax.ShapeDtypeStruct((b, h, l, QK_PAD_DIM), BF16),
            jax.ShapeDtypeStruct((b, h, l, QK_PAD_DIM), BF16),
            jax.ShapeDtypeStruct((b, h, l, V_HEAD_DIM), BF16),
        ],
        compiler_params=_params("parallel", "parallel"),
        name="mla_pre",
    )(x, gmix, w_in_p, qag, kvag, w_qb_p, w_kvb, gq_p, gk_p, cos_t, sin_lo_t, sin_hi_t)


def _flash_kernel(q_ref, k_ref, v_ref, o_ref, m_ref, l_ref, acc_ref):
    kv = pl.program_id(3)

    @pl.when(kv == 0)
    def _():
        m_ref[...] = jnp.full_like(m_ref, -jnp.inf)
        l_ref[...] = jnp.zeros_like(l_ref)
        acc_ref[...] = jnp.zeros_like(acc_ref)

    s = _dot_nt(q_ref[...], k_ref[...])
    m_prev = m_ref[...]
    m_new = jnp.maximum(m_prev, jnp.max(s, axis=-1, keepdims=True))
    alpha = jnp.exp2(m_prev - m_new)
    p = jnp.exp2(s - m_new)
    l_ref[...] = alpha * l_ref[...] + jnp.sum(p, axis=-1, keepdims=True)
    acc_ref[...] = alpha * acc_ref[...] + _dot(p.astype(BF16), v_ref[...])
    m_ref[...] = m_new

    @pl.when(kv == pl.num_programs(3) - 1)
    def _():
        o_ref[...] = (acc_ref[...] / l_ref[...]).astype(o_ref.dtype)


def _flash_bounded_kernel(q_ref, k_ref, v_ref, o_ref, acc_ref, *, tk):
    dv = v_ref.shape[-1]
    q = q_ref[...]
    ones = jnp.ones((tk, dv), BF16)
    for c in range(k_ref.shape[0] // tk):
        rows = slice(c * tk, (c + 1) * tk)
        p = jnp.exp2(_dot_nt(q, k_ref[rows, :])).astype(BF16)
        pv = _dot(p, jnp.concatenate([v_ref[rows, :], ones], axis=-1))
        if c == 0:
            acc_ref[...] = pv
        else:
            acc_ref[...] += pv
    acc = acc_ref[...]
    o_ref[...] = (acc[:, :dv] / acc[:, dv:]).astype(o_ref.dtype)


def flash_attention_bounded(q, k, v, tq, tk):
    b, h, l, dk = q.shape
    dv = v.shape[-1]
    return pl.pallas_call(
        functools.partial(_flash_bounded_kernel, tk=tk),
        grid=(b, h, l // tq),
        in_specs=[
            pl.BlockSpec((None, None, tq, dk), lambda i, j, a: (i, j, a, 0)),
            pl.BlockSpec((None, None, l, dk), lambda i, j, a: (i, j, 0, 0)),
            pl.BlockSpec((None, None, l, dv), lambda i, j, a: (i, j, 0, 0)),
        ],
        out_specs=pl.BlockSpec((None, tq, dv), lambda i, j, a: (i, a, j)),
        out_shape=jax.ShapeDtypeStruct((b, l, h * dv), BF16),
        scratch_shapes=[pltpu.VMEM((tq, 2 * dv), F32)],
        compiler_params=_params("parallel", "parallel", "arbitrary"),
        name="flash_attention_bounded",
    )(q, k, v)


def flash_attention(q, k, v, tq, tk):
    b, h, l, dk = q.shape
    dv = v.shape[-1]
    return pl.pallas_call(
        _flash_kernel,
        grid=(b, h, l // tq, l // tk),
        in_specs=[
            pl.BlockSpec((None, None, tq, dk), lambda i, j, a, c: (i, j, a, 0)),
            pl.BlockSpec((None, None, tk, dk), lambda i, j, a, c: (i, j, c, 0)),
            pl.BlockSpec((None, None, tk, dv), lambda i, j, a, c: (i, j, c, 0)),
        ],
        out_specs=pl.BlockSpec((None, tq, dv), lambda i, j, a, c: (i, a, j)),
        out_shape=jax.ShapeDtypeStruct((b, l, h * dv), BF16),
        scratch_shapes=[pltpu.VMEM((tq, 1), F32), pltpu.VMEM((tq, 1), F32), pltpu.VMEM((tq, dv), F32)],
        compiler_params=_params("parallel", "parallel", "parallel", "arbitrary"),
        name="flash_attention",
    )(q, k, v)


def _hgrn_scan_kernel(*refs, reverse, n_chunks):
    if reverse:
        q_ref, z_ref, v_ref, lb_ref, sel_ref, of_ref, gate_ref, ong_ref, o_ref, s_ref, b_ref, kk_ref, lhs_ref = refs
    else:
        q_ref, z_ref, v_ref, lb_ref, sel_ref, o_ref, s_ref, b_ref, kk_ref, lhs_ref = refs
    c = HGRN_CHUNK
    sub = HGRN_SUB
    per = c // sub
    kd = HGRN_KEY_DIM
    tl = n_chunks * c

    @pl.when(pl.program_id(2) == 0)
    def _():
        s_ref[...] = jnp.zeros_like(s_ref)

    lb = lb_ref[...]
    z = z_ref[...]
    e = jnp.exp(-jnp.abs(z))
    log_sig = jnp.minimum(z, 0.0) - jnp.log1p(e)
    a = jnp.log(lb)
    cterm = jnp.log1p(-lb) + log_sig
    log_f = jnp.maximum(a, cterm) + jnp.log1p(jnp.exp(-jnp.abs(a - cterm)))
    kk = (1.0 - lb) * (jnp.where(z >= 0.0, e, 1.0) / (1.0 + e))

    row = lax.broadcasted_iota(jnp.int32, (c, c), 0)
    col = lax.broadcasted_iota(jnp.int32, (c, c), 1)
    visible = (col >= row) if reverse else (col <= row)
    tri_f = visible.astype(F32)
    log2_f = log_f * LOG2_E
    for ci in range(n_chunks):
        bsum = jnp.dot(tri_f, log2_f[ci * c:(ci + 1) * c], precision=lax.Precision.HIGHEST,
                       preferred_element_type=F32)
        b_ref[ci * per:(ci + 1) * per] = bsum.reshape(per, sub, kd)
    kk_ref[...] = kk.reshape(tl // sub, sub, kd)

    q3 = q_ref[...].reshape(tl // sub, sub, kd)
    b3 = b_ref[...]
    for s in range(sub):
        k_row = kk_ref[:, s:s + 1, :]
        b_row = b_ref[:, s:s + 1, :]
        w = q3 * k_row * jnp.exp2(jnp.minimum(b3 - b_row, 0.0))
        lhs_ref[:, s * kd:(s + 1) * kd] = w.reshape(tl, kd).astype(BF16)
    d_all = _dot(lhs_ref[...], sel_ref[...])

    same_sub = (row // sub) == (col // sub)
    same_half = (row // (2 * sub)) == (col // (2 * sub))
    mask_d = jnp.where(same_sub & visible, 1.0, 0.0)
    mask_b = jnp.where(same_half, 1.0, 0.0)
    pos = lax.broadcasted_iota(jnp.int32, (c, kd), 0)
    if reverse:
        is_q_a = pos < c // 2
        is_q_b = (pos % (2 * sub)) < sub
        ref_a, ref_b0, ref_b1 = c // 2, sub, c // 2 + sub
    else:
        is_q_a = pos >= c // 2
        is_q_b = (pos % (2 * sub)) >= sub
        ref_a, ref_b0, ref_b1 = c // 2 - 1, sub - 1, c // 2 + sub - 1

    def split_scores(qc, kc, bc, b_split, is_q):
        dec = jnp.exp2(jnp.where(is_q, bc - b_split, b_split - bc))
        qd = jnp.where(is_q, qc * dec, 0.0).astype(BF16)
        kd_ = jnp.where(is_q, 0.0, kc * dec).astype(BF16)
        return _dot_nt(qd, kd_)

    order = range(n_chunks - 1, -1, -1) if reverse else range(n_chunks)
    edge = 0 if reverse else c - 1
    for ci in order:
        rows = slice(ci * c, (ci + 1) * c)
        bc = b_ref[ci * per:(ci + 1) * per].reshape(c, kd)
        qc = q_ref[rows, :]
        kc = kk_ref[ci * per:(ci + 1) * per].reshape(c, kd)
        vc = v_ref[rows, :]
        b_half = jnp.concatenate([jnp.broadcast_to(bc[ref_b0:ref_b0 + 1, :], (c // 2, kd)),
                                  jnp.broadcast_to(bc[ref_b1:ref_b1 + 1, :], (c // 2, kd))], axis=0)
        att = (d_all[rows, :] * mask_d + split_scores(qc, kc, bc, bc[ref_a:ref_a + 1, :], is_q_a)
               + split_scores(qc, kc, bc, b_half, is_q_b) * mask_b)
        b_edge = bc[edge:edge + 1, :]
        st = s_ref[...]
        o_intra = _dot(att.astype(BF16), vc.astype(BF16))
        o_inter = _dot_nt((qc * jnp.exp2(bc)).astype(BF16), st.astype(BF16))
        k_dec = (kc * jnp.exp2(b_edge - bc)).astype(BF16)
        s_ref[...] = st * jnp.exp2(b_edge) + _dot(vc.T.astype(BF16), k_dec)
        o = o_intra + o_inter
        if reverse:
            o = _rms(o + of_ref[rows, :], ong_ref[...])
            g = gate_ref[rows, :]
            o_ref[rows, :] = (o * (g * jax.nn.sigmoid(g))).astype(o_ref.dtype)
        else:
            o_ref[rows, :] = o


def hgrn_scan(proj, lb, reverse, tl, o_fwd=None, o_norm_g=None):
    b, l, _ = proj.shape
    h = HGRN_HEADS
    nl = l // tl
    n_chunks = tl // HGRN_CHUNK
    kd = HGRN_KEY_DIM

    def pos(j):
        return (nl - 1 - j) if reverse else j

    def col_spec(section):
        return pl.BlockSpec((None, tl, kd), lambda i, hh, j: (i, pos(j), section * h + hh))

    sel_s = lax.broadcasted_iota(jnp.int32, (HGRN_SUB, kd, HGRN_CHUNK), 0)
    sel_j = lax.broadcasted_iota(jnp.int32, (HGRN_SUB, kd, HGRN_CHUNK), 2)
    sel = (sel_j % HGRN_SUB == sel_s).astype(BF16).reshape(HGRN_SUB * kd, HGRN_CHUNK)

    in_specs = [col_spec(0), col_spec(2 if reverse else 1), col_spec(3),
                pl.BlockSpec((1, kd), lambda i, hh, j: (0, hh)), _resident(sel.shape)]
    args = [proj, proj, proj, lb, sel]
    out_spec = pl.BlockSpec((None, tl, kd), lambda i, hh, j: (i, pos(j), hh))
    if reverse:
        in_specs += [out_spec, col_spec(4), _resident((1, HGRN_VAL_DIM))]
        args += [o_fwd, proj, o_norm_g]
    return pl.pallas_call(
        functools.partial(_hgrn_scan_kernel, reverse=reverse, n_chunks=n_chunks),
        grid=(b, h, nl),
        in_specs=in_specs,
        out_specs=out_spec,
        out_shape=jax.ShapeDtypeStruct((b, l, D_MODEL), BF16 if reverse else F32),
        scratch_shapes=[
            pltpu.VMEM((HGRN_VAL_DIM, kd), F32),
            pltpu.VMEM((tl // HGRN_SUB, HGRN_SUB, kd), F32),
            pltpu.VMEM((tl // HGRN_SUB, HGRN_SUB, kd), F32),
            pltpu.VMEM((tl, HGRN_SUB * kd), BF16),
        ],
        compiler_params=_params("parallel", "parallel", "arbitrary"),
        name="hgrn_scan_bwd" if reverse else "hgrn_scan_fwd",
    )(*args)


def _fnet_channel_kernel(x_ref, g_ref, cn_ref, sn_ref, u_ref, w_ref):
    xn = _rms(x_ref[...], g_ref[...]).astype(BF16)
    for gi in range(FNET_GROUPS):
        cols = slice(gi * FNET_GROUP_DIM, (gi + 1) * FNET_GROUP_DIM)
        u_ref[:, cols] = _dot(xn[:, cols], cn_ref[...]).astype(BF16)
        w_ref[:, cols] = _dot(xn[:, cols], sn_ref[...]).astype(BF16)


def fnet_channel(x, g, cn, sn, tm):
    t, d = x.shape
    spec = pl.BlockSpec((tm, d), lambda i: (i, 0))
    return pl.pallas_call(
        _fnet_channel_kernel,
        grid=(t // tm,),
        in_specs=[spec, _resident((1, d)), _resident(cn.shape), _resident(sn.shape)],
        out_specs=[spec, spec],
        out_shape=[jax.ShapeDtypeStruct((t, d), BF16)] * 2,
        compiler_params=_params("parallel"),
        name="fnet_channel",
    )(x, g, cn, sn)


def _fnet_seq_kernel(cl_ref, nsl_ref, u_ref, w_ref, x_ref, wo_ref, o_ref, acc_ref):
    k = pl.program_id(2)

    @pl.when(k == 0)
    def _():
        acc_ref[...] = jnp.zeros_like(acc_ref)

    acc_ref[...] += _dot(cl_ref[...], u_ref[...]) + _dot(nsl_ref[...], w_ref[...])

    @pl.when(k == pl.num_programs(2) - 1)
    def _():
        o_ref[...] = x_ref[...] + _dot(acc_ref[...].astype(BF16), wo_ref[...])


def fnet_seq(cl, nsl, u, w, x, w_o, tm, tk):
    b, l, d = x.shape
    spec_dft = pl.BlockSpec((tm, tk), lambda i, j, k: (j, k))
    spec_uw = pl.BlockSpec((None, tk, d), lambda i, j, k: (i, k, 0))
    spec_x = pl.BlockSpec((None, tm, d), lambda i, j, k: (i, j, 0))
    return pl.pallas_call(
        _fnet_seq_kernel,
        grid=(b, l // tm, l // tk),
        in_specs=[spec_dft, spec_dft, spec_uw, spec_uw, spec_x, _resident(w_o.shape)],
        out_specs=spec_x,
        out_shape=jax.ShapeDtypeStruct((b, l, d), F32),
        scratch_shapes=[pltpu.VMEM((tm, d), F32)],
        compiler_params=_params("parallel", "parallel", "arbitrary"),
        name="fnet_seq",
    )(cl, nsl, u, w, x, w_o)


def _dft_tables(n):
    idx = jnp.arange(n, dtype=jnp.int32)
    jk = (idx[:, None] * idx[None, :]) % n
    ang = jk.astype(F32) * (2.0 * jnp.pi / n)
    scale = n ** -0.5
    return jnp.cos(ang) * scale, jnp.sin(ang) * scale


def _rope_tables(length):
    dim = QK_ROPE_DIM
    half = dim // 2
    inv = 1.0 / (ROPE_THETA ** (jnp.arange(0, dim, 2, dtype=F32) / dim))
    ang = jnp.arange(length, dtype=F32)[:, None] * inv[None, :]
    cos, sin = jnp.cos(ang), jnp.sin(ang)
    zero = jnp.zeros_like(cos)
    cos_t = jnp.concatenate([cos, cos, zero, zero], axis=-1)
    sin_lo = jnp.concatenate([-sin, zero, zero, zero], axis=-1)
    sin_hi = jnp.concatenate([zero, sin, zero, zero], axis=-1)
    return cos_t, sin_lo, sin_hi


def _pad_cols(a, n):
    return jnp.concatenate([a, jnp.zeros(a.shape[:-1] + (n,), a.dtype)], axis=-1)


def _row(v):
    return v.reshape(1, -1).astype(F32)


def _tile(n, cap):
    return min(n, cap)


def kernel(x_prompt, x_sample, mem_prompt, mem_sample, norm_mix_g, mla_w_in, mla_q_a_norm_g, mla_kv_a_norm_g, mla_w_q_b, mla_w_kv_b, mla_q_norm_g, mla_k_norm_g, mla_w_o, hgrn_w_in, hgrn_lb_logits, hgrn_o_norm_g, hgrn_w_o, fnet_w_o, norm_xq_g, norm_mem_g, mem_w_q, mem_w_kv, mem_q_norm_g, mem_k_norm_g, mem_w_o, norm_mlp_g, mlp_w1, mlp_w2):
    depth = norm_mix_g.shape[0]
    n_mixers = 3

    lb_all = jnp.cumsum(jax.nn.softmax(hgrn_lb_logits.astype(F32), axis=0), axis=0)
    lb_all = lb_all - lb_all[:1]

    mla_w_in_p = _pad_cols(mla_w_in, LANE - QK_ROPE_DIM).astype(BF16)
    n_mla = mla_w_in.shape[0]
    wqb = mla_w_q_b.reshape(n_mla, Q_LORA_RANK, MLA_HEADS, QK_HEAD_DIM)
    mla_w_qb_p = _pad_cols(wqb, QK_PAD_DIM - QK_HEAD_DIM).reshape(n_mla, Q_LORA_RANK, MLA_HEADS * QK_PAD_DIM).astype(BF16)
    mla_w_kvb = mla_w_kv_b.astype(BF16)
    mla_gq_p = _pad_cols(mla_q_norm_g * (QK_HEAD_DIM ** -0.5 * LOG2_E), QK_PAD_DIM - QK_HEAD_DIM)
    mla_score_bound = (QK_HEAD_DIM ** 0.5 * LOG2_E * 1.02) * (
        jnp.max(jnp.abs(mla_q_norm_g), axis=-1) * jnp.max(jnp.abs(mla_k_norm_g), axis=-1))
    mla_gk_p = _pad_cols(mla_k_norm_g, QK_PAD_DIM - QK_HEAD_DIM)
    mla_w_o_b = mla_w_o.astype(BF16)
    hgrn_w_in_b = hgrn_w_in.astype(BF16)
    hgrn_w_o_b = hgrn_w_o.astype(BF16)
    fnet_w_o_b = fnet_w_o.astype(BF16)
    mem_w_q_b = mem_w_q.astype(BF16)
    mem_w_kv_b = mem_w_kv.astype(BF16)
    mem_w_o_b = mem_w_o.astype(BF16)
    mem_qg_scaled = mem_q_norm_g * (MEM_HEAD_DIM ** -0.5)
    mlp_w1_b = mlp_w1.astype(BF16)
    mlp_w2_b = mlp_w2.astype(BF16)

    has_fnet = depth >= n_mixers
    if has_fnet:
        cn, sn = _dft_tables(FNET_GROUP_DIM)
        cn, sn = cn.astype(BF16), sn.astype(BF16)

    def run_trunk(x, mem):
        b, l, d = x.shape
        t = b * l
        tm = _tile(l, 512)
        if depth >= 1:
            rope_tabs = _rope_tables(l)
        if has_fnet:
            cl, sl = _dft_tables(l)
            cl, nsl = cl.astype(BF16), (-sl).astype(BF16)
        for i in range(depth):
            kind = i % n_mixers
            j = i // n_mixers
            gmix = _row(norm_mix_g[i])
            if kind == 0:
                q, k, v = mla_pre(x, gmix, mla_w_in_p[j], _row(mla_q_a_norm_g[j]), _row(mla_kv_a_norm_g[j]),
                                  mla_w_qb_p[j], mla_w_kvb[j], _row(mla_gq_p[j]), _row(mla_gk_p[j]),
                                  *rope_tabs, tm=_tile(l, 256))
                tile = _tile(l, 1024)
                o = lax.cond(mla_score_bound[j] <= SCORE_LOG2_LIMIT,
                             functools.partial(flash_attention_bounded, tq=tile, tk=tile),
                             functools.partial(flash_attention, tq=tile, tk=tile), q, k, v)
                x = matmul_residual(o.reshape(t, d), mla_w_o_b[j], x.reshape(t, d), tm).reshape(b, l, d)
            elif kind == 1:
                proj = norm_matmul(x.reshape(t, d), gmix, hgrn_w_in_b[j], F32, tm, 2048).reshape(b, l, -1)
                tl = _tile(l, 512)
                o_f = hgrn_scan(proj, _row(lb_all[i, 0]), False, tl)
                y = hgrn_scan(proj, _row(lb_all[i, 1]), True, tl, o_fwd=o_f, o_norm_g=_row(hgrn_o_norm_g[j]))
                x = matmul_residual(y.reshape(t, d), hgrn_w_o_b[j], x.reshape(t, d), tm).reshape(b, l, d)
            else:
                u, w = fnet_channel(x.reshape(t, d), gmix, cn, sn, tm)
                x = fnet_seq(cl, nsl, u.reshape(b, l, d), w.reshape(b, l, d), x, fnet_w_o_b[j], tm, tm)
            mk, mv = mem_kv(mem, _row(norm_mem_g[i]), mem_w_kv_b[i], _row(mem_k_norm_g[i]))
            x = mem_xattn(x, _row(norm_xq_g[i]), mem_w_q_b[i], _row(mem_qg_scaled[i]), mk, mv, mem_w_o_b[i], tm)
            x = mlp(x.reshape(t, d), _row(norm_mlp_g[i]), mlp_w1_b[i], mlp_w2_b[i], tm, 512).reshape(b, l, d)
        return x

    return (run_trunk(x_prompt, mem_prompt), run_trunk(x_sample, mem_sample))
```

```python
import functools
from typing import NamedTuple

import jax
import jax.numpy as jnp
from jax import lax
from jax.experimental import pallas as pl
from jax.experimental.pallas import tpu as pltpu

F32 = jnp.float32
BF16 = jnp.bfloat16

EPS = 1e-6
D_MODEL = 2048
D_FF = 4 * D_MODEL

MLA_HEADS = 16
Q_LORA_RANK = 512
KV_LORA_RANK = 512
QK_NOPE_DIM = 128
QK_ROPE_DIM = 64
QK_HEAD_DIM = QK_NOPE_DIM + QK_ROPE_DIM
V_HEAD_DIM = 128
ROPE_THETA = 10000.0
QK_PAD_DIM = 256
SCORE_LOG2_LIMIT = 100.0
ONES_ROWS = 16

HGRN_HEADS = 16
HGRN_KEY_DIM = 128
HGRN_VAL_DIM = D_MODEL // HGRN_HEADS
HGRN_KEY_WIDTH = HGRN_HEADS * HGRN_KEY_DIM
HGRN_CHUNK = 64
HGRN_SUB = 8
LOG2_E = 1.4426950408889634

FNET_GROUPS = 4
FNET_GROUP_DIM = D_MODEL // FNET_GROUPS

MEM_HEADS = 4
MEM_HEAD_DIM = 128
MEM_WIDTH = MEM_HEADS * MEM_HEAD_DIM

LANE = 128
VMEM_LIMIT_BYTES = 56 * 1024 * 1024


def _params(*semantics):
    return pltpu.CompilerParams(dimension_semantics=semantics, vmem_limit_bytes=VMEM_LIMIT_BYTES)


def _resident(shape):
    nd = len(shape)
    return pl.BlockSpec(shape, lambda *_: (0,) * nd, pipeline_mode=pl.Buffered(1))


def _rms(x, g):
    ms = jnp.mean(x * x, axis=-1, keepdims=True)
    return x * lax.rsqrt(ms + EPS) * g


def _dot(a, b):
    return jnp.dot(a, b, preferred_element_type=F32)


def _dot_nt(a, b):
    return lax.dot_general(a, b, (((1,), (1,)), ((), ())), preferred_element_type=F32)


def _norm_matmul_kernel(x_ref, g_ref, w_ref, o_ref, xn_ref):
    @pl.when(pl.program_id(1) == 0)
    def _():
        xn_ref[...] = _rms(x_ref[...], g_ref[...]).astype(BF16)

    o_ref[...] = _dot(xn_ref[...], w_ref[...]).astype(o_ref.dtype)


def norm_matmul(x, g, w, out_dtype, tm, tn):
    t, d = x.shape
    n = w.shape[1]
    return pl.pallas_call(
        _norm_matmul_kernel,
        grid=(t // tm, n // tn),
        in_specs=[
            pl.BlockSpec((tm, d), lambda i, j: (i, 0)),
            _resident((1, d)),
            pl.BlockSpec((d, tn), lambda i, j: (0, j)),
        ],
        out_specs=pl.BlockSpec((tm, tn), lambda i, j: (i, j)),
        out_shape=jax.ShapeDtypeStruct((t, n), out_dtype),
        scratch_shapes=[pltpu.VMEM((tm, d), BF16)],
        compiler_params=_params("parallel", "arbitrary"),
        name="norm_matmul",
    )(x, g, w)


def _matmul_residual_kernel(a_ref, w_ref, r_ref, o_ref):
    o_ref[...] = r_ref[...] + _dot(a_ref[...], w_ref[...])


def matmul_residual(a, w, res, tm):
    t, k = a.shape
    n = w.shape[1]
    return pl.pallas_call(
        _matmul_residual_kernel,
        grid=(t // tm,),
        in_specs=[
            pl.BlockSpec((tm, k), lambda i: (i, 0)),
            _resident((k, n)),
            pl.BlockSpec((tm, n), lambda i: (i, 0)),
        ],
        out_specs=pl.BlockSpec((tm, n), lambda i: (i, 0)),
        out_shape=jax.ShapeDtypeStruct((t, n), F32),
        compiler_params=_params("parallel"),
        name="matmul_residual",
    )(a, w, res)


def _mlp_kernel(x_ref, g_ref, w1_ref, w2_ref, o_ref, xn_ref, acc_ref):
    j = pl.program_id(1)

    @pl.when(j == 0)
    def _():
        xn_ref[...] = _rms(x_ref[...], g_ref[...]).astype(BF16)
        acc_ref[...] = jnp.zeros_like(acc_ref)

    h = jnp.maximum(_dot(xn_ref[...], w1_ref[...]), 0.0)
    acc_ref[...] += _dot((h * h).astype(BF16), w2_ref[...])

    @pl.when(j == pl.num_programs(1) - 1)
    def _():
        o_ref[...] = x_ref[...] + acc_ref[...]


def mlp(x, g, w1, w2, tm, tf):
    t, d = x.shape
    f = w1.shape[1]
    return pl.pallas_call(
        _mlp_kernel,
        grid=(t // tm, f // tf),
        in_specs=[
            pl.BlockSpec((tm, d), lambda i, j: (i, 0)),
            _resident((1, d)),
            pl.BlockSpec((d, tf), lambda i, j: (0, j)),
            pl.BlockSpec((tf, d), lambda i, j: (j, 0)),
        ],
        out_specs=pl.BlockSpec((tm, d), lambda i, j: (i, 0)),
        out_shape=jax.ShapeDtypeStruct((t, d), F32),
        scratch_shapes=[pltpu.VMEM((tm, d), BF16), pltpu.VMEM((tm, d), F32)],
        compiler_params=_params("parallel", "arbitrary"),
        name="mlp",
    )(x, g, w1, w2)


def _mem_kv_kernel(m_ref, g_ref, w_ref, kg_ref, k_out, v_out):
    mn = _rms(m_ref[...], g_ref[...]).astype(BF16)
    kv = _dot(mn, w_ref[...])
    kg = kg_ref[...]
    for h in range(MEM_HEADS):
        lo = h * MEM_HEAD_DIM
        k_out[:, lo:lo + MEM_HEAD_DIM] = _rms(kv[:, lo:lo + MEM_HEAD_DIM], kg).astype(BF16)
    v_out[...] = kv[:, MEM_WIDTH:].astype(BF16)


def mem_kv(mem, g, w_kv, k_g):
    b, m, d = mem.shape
    spec_out = pl.BlockSpec((None, m, MEM_WIDTH), lambda i: (i, 0, 0))
    return pl.pallas_call(
        _mem_kv_kernel,
        grid=(b,),
        in_specs=[
            pl.BlockSpec((None, m, d), lambda i: (i, 0, 0)),
            _resident((1, d)),
            _resident((d, 2 * MEM_WIDTH)),
            _resident((1, MEM_HEAD_DIM)),
        ],
        out_specs=[spec_out, spec_out],
        out_shape=[jax.ShapeDtypeStruct((b, m, MEM_WIDTH), BF16)] * 2,
        compiler_params=_params("parallel"),
        name="mem_kv",
    )(mem, g, w_kv, k_g)


def _mem_xattn_kernel(x_ref, g_ref, wq_ref, qg_ref, k_ref, v_ref, wo_ref, o_ref):
    x = x_ref[...]
    q = _dot(_rms(x, g_ref[...]).astype(BF16), wq_ref[...])
    qg = qg_ref[...]
    heads = []
    for h in range(MEM_HEADS):
        lo = h * MEM_HEAD_DIM
        qh = _rms(q[:, lo:lo + MEM_HEAD_DIM], qg).astype(BF16)
        s = _dot_nt(qh, k_ref[:, lo:lo + MEM_HEAD_DIM])
        e = jnp.exp(s - jnp.max(s, axis=-1, keepdims=True))
        p = e / jnp.sum(e, axis=-1, keepdims=True)
        heads.append(_dot(p.astype(BF16), v_ref[:, lo:lo + MEM_HEAD_DIM]))
    o = jnp.concatenate(heads, axis=-1).astype(BF16)
    o_ref[...] = x + _dot(o, wo_ref[...])


def mem_xattn(x, g, w_q, q_g_scaled, k, v, w_o, tm):
    b, l, d = x.shape
    m = k.shape[1]
    spec_x = pl.BlockSpec((None, tm, d), lambda i, j: (i, j, 0))
    spec_kv = pl.BlockSpec((None, m, MEM_WIDTH), lambda i, j: (i, 0, 0))
    return pl.pallas_call(
        _mem_xattn_kernel,
        grid=(b, l // tm),
        in_specs=[
            spec_x,
            _resident((1, d)),
            _resident((d, MEM_WIDTH)),
            _resident((1, MEM_HEAD_DIM)),
            spec_kv,
            spec_kv,
            _resident((MEM_WIDTH, d)),
        ],
        out_specs=spec_x,
        out_shape=jax.ShapeDtypeStruct((b, l, d), F32),
        compiler_params=_params("parallel", "parallel"),
        name="mem_xattn",
    )(x, g, w_q, q_g_scaled, k, v, w_o)


def _mla_pre_kernel(x_ref, gmix_ref, win_ref, qag_ref, kvag_ref, wqb_ref, wkvb_ref, gq_ref, gk_ref,
                    cos_ref, sin_lo_ref, sin_hi_ref, q_out, k_out, v_out):
    xn = _rms(x_ref[...], gmix_ref[...]).astype(BF16)
    proj = _dot(xn, win_ref[...])
    ql = _rms(proj[:, :Q_LORA_RANK], qag_ref[...]).astype(BF16)
    kvl = _rms(proj[:, Q_LORA_RANK:Q_LORA_RANK + KV_LORA_RANK], kvag_ref[...]).astype(BF16)
    kr = proj[:, Q_LORA_RANK + KV_LORA_RANK:]
    qf = _dot(ql, wqb_ref[...])
    kvf = _dot(kvl, wkvb_ref[...])
    cos = cos_ref[...]
    sin_lo = sin_lo_ref[...]
    sin_hi = sin_hi_ref[...]

    def rope(r):
        half = QK_ROPE_DIM // 2
        return r * cos + pltpu.roll(r, half, 1) * sin_hi + pltpu.roll(r, LANE - half, 1) * sin_lo

    gq = gq_ref[...]
    gk = gk_ref[...]
    inv_dim = 1.0 / QK_HEAD_DIM
    ss_kr = jnp.sum(kr * kr, axis=-1, keepdims=True)
    kr_rot = rope(kr * gk[:, QK_NOPE_DIM:])
    for h in range(MLA_HEADS):
        lo = h * QK_PAD_DIM
        qh = qf[:, lo:lo + QK_PAD_DIM]
        inv_q = lax.rsqrt(jnp.sum(qh * qh, axis=-1, keepdims=True) * inv_dim + EPS)
        qn = qh * inv_q * gq
        q_out[h, :, :QK_NOPE_DIM] = qn[:, :QK_NOPE_DIM].astype(BF16)
        q_out[h, :, QK_NOPE_DIM:] = rope(qn[:, QK_NOPE_DIM:]).astype(BF16)
        kn = kvf[:, lo:lo + QK_NOPE_DIM]
        inv_k = lax.rsqrt((jnp.sum(kn * kn, axis=-1, keepdims=True) + ss_kr) * inv_dim + EPS)
        k_out[h, :, :QK_NOPE_DIM] = (kn * inv_k * gk[:, :QK_NOPE_DIM]).astype(BF16)
        k_out[h, :, QK_NOPE_DIM:] = (kr_rot * inv_k).astype(BF16)
        v_out[h] = kvf[:, lo + QK_NOPE_DIM:lo + QK_PAD_DIM].T.astype(BF16)


def mla_pre(x, gmix, w_in_p, qag, kvag, w_qb_p, w_kvb, gq_p, gk_p, cos_t, sin_lo_t, sin_hi_t, tm):
    b, l, d = x.shape
    h = MLA_HEADS
    spec_tab = pl.BlockSpec((tm, LANE), lambda i, j: (j, 0))
    spec_qk = pl.BlockSpec((None, h, tm, QK_PAD_DIM), lambda i, j: (i, 0, j, 0))
    spec_v = pl.BlockSpec((None, h, V_HEAD_DIM, tm), lambda i, j: (i, 0, 0, j))
    return pl.pallas_call(
        _mla_pre_kernel,
        grid=(b, l // tm),
        in_specs=[
            pl.BlockSpec((None, tm, d), lambda i, j: (i, j, 0)),
            _resident((1, d)),
            _resident(w_in_p.shape),
            _resident((1, Q_LORA_RANK)),
            _resident((1, KV_LORA_RANK)),
            _resident(w_qb_p.shape),
            _resident(w_kvb.shape),
            _resident((1, QK_PAD_DIM)),
            _resident((1, QK_PAD_DIM)),
            spec_tab, spec_tab, spec_tab,
        ],
        out_specs=[spec_qk, spec_qk, spec_v],
        out_shape=[
            jax.ShapeDtypeStruct((b, h, l, QK_PAD_DIM), BF16),
            jax.ShapeDtypeStruct((b, h, l, QK_PAD_DIM), BF16),
            jax.ShapeDtypeStruct((b, h, V_HEAD_DIM, l), BF16),
        ],
        compiler_params=_params("parallel", "parallel"),
        name="mla_pre",
    )(x, gmix, w_in_p, qag, kvag, w_qb_p, w_kvb, gq_p, gk_p, cos_t, sin_lo_t, sin_hi_t)


def _flash_kernel(q_ref, k_ref, vt_ref, o_ref, m_ref, l_ref, acc_ref):
    kv = pl.program_id(3)

    @pl.when(kv == 0)
    def _():
        m_ref[...] = jnp.full_like(m_ref, -jnp.inf)
        l_ref[...] = jnp.zeros_like(l_ref)
        acc_ref[...] = jnp.zeros_like(acc_ref)

    s = _dot_nt(k_ref[...], q_ref[...])
    m_prev = m_ref[...]
    m_new = jnp.maximum(m_prev, jnp.max(s, axis=0, keepdims=True))
    alpha = jnp.exp2(m_prev - m_new)
    p = jnp.exp2(s - m_new)
    l_ref[...] = alpha * l_ref[...] + jnp.sum(p, axis=0, keepdims=True)
    acc_ref[...] = alpha * acc_ref[...] + _dot(vt_ref[...], p.astype(BF16))
    m_ref[...] = m_new

    @pl.when(kv == pl.num_programs(3) - 1)
    def _():
        o_ref[...] = (acc_ref[...] / l_ref[...]).T.astype(o_ref.dtype)


def flash_attention(q, k, vt, tq, tk):
    b, h, l, dk = q.shape
    dv = vt.shape[2]
    return pl.pallas_call(
        _flash_kernel,
        grid=(b, h, l // tq, l // tk),
        in_specs=[
            pl.BlockSpec((None, None, tq, dk), lambda i, j, a, c: (i, j, a, 0)),
            pl.BlockSpec((None, None, tk, dk), lambda i, j, a, c: (i, j, c, 0)),
            pl.BlockSpec((None, None, dv, tk), lambda i, j, a, c: (i, j, 0, c)),
        ],
        out_specs=pl.BlockSpec((None, tq, dv), lambda i, j, a, c: (i, a, j)),
        out_shape=jax.ShapeDtypeStruct((b, l, h * dv), BF16),
        scratch_shapes=[pltpu.VMEM((1, tq), F32), pltpu.VMEM((1, tq), F32), pltpu.VMEM((dv, tq), F32)],
        compiler_params=_params("parallel", "parallel", "parallel", "arbitrary"),
        name="flash_attention",
    )(q, k, vt)


def _flash_bounded_kernel(q_ref, k_ref, vt_ref, o_ref, acc_ref, *, tk):
    dv = vt_ref.shape[0]
    q = q_ref[...]
    ones = jnp.ones((ONES_ROWS, tk), BF16)
    for c in range(k_ref.shape[0] // tk):
        cols = slice(c * tk, (c + 1) * tk)
        p = jnp.exp2(_dot_nt(k_ref[cols, :], q)).astype(BF16)
        pv = _dot(jnp.concatenate([vt_ref[:, cols], ones], axis=0), p)
        if c == 0:
            acc_ref[...] = pv
        else:
            acc_ref[...] += pv
    acc = acc_ref[...]
    o_ref[...] = (acc[:dv] / acc[dv:dv + 1]).T.astype(o_ref.dtype)


def flash_attention_bounded(q, k, vt, tq, tk):
    b, h, l, dk = q.shape
    dv = vt.shape[2]
    return pl.pallas_call(
        functools.partial(_flash_bounded_kernel, tk=tk),
        grid=(b, h, l // tq),
        in_specs=[
            pl.BlockSpec((None, None, tq, dk), lambda i, j, a: (i, j, a, 0)),
            pl.BlockSpec((None, None, l, dk), lambda i, j, a: (i, j, 0, 0)),
            pl.BlockSpec((None, None, dv, l), lambda i, j, a: (i, j, 0, 0)),
        ],
        out_specs=pl.BlockSpec((None, tq, dv), lambda i, j, a: (i, a, j)),
        out_shape=jax.ShapeDtypeStruct((b, l, h * dv), BF16),
        scratch_shapes=[pltpu.VMEM((dv + ONES_ROWS, tq), F32)],
        compiler_params=_params("parallel", "parallel", "arbitrary"),
        name="flash_attention_bounded",
    )(q, k, vt)


def _hgrn_scan_kernel(*refs, reverse, n_chunks):
    if reverse:
        q_ref, z_ref, v_ref, lb_ref, sel_ref, of_ref, gate_ref, ong_ref, o_ref, s_ref, b_ref, kk_ref, lhs_ref = refs
    else:
        q_ref, z_ref, v_ref, lb_ref, sel_ref, o_ref, s_ref, b_ref, kk_ref, lhs_ref = refs
    c = HGRN_CHUNK
    sub = HGRN_SUB
    per = c // sub
    kd = HGRN_KEY_DIM
    tl = n_chunks * c

    @pl.when(pl.program_id(2) == 0)
    def _():
        s_ref[...] = jnp.zeros_like(s_ref)

    lb = lb_ref[...]
    z = z_ref[...]
    e = jnp.exp(-jnp.abs(z))
    log_sig = jnp.minimum(z, 0.0) - jnp.log(1.0 + e)
    a = jnp.log(lb)
    cterm = jnp.log1p(-lb) + log_sig
    log_f = jnp.maximum(a, cterm) + jnp.log(1.0 + jnp.exp(-jnp.abs(a - cterm)))
    kk = (1.0 - lb) * (jnp.where(z >= 0.0, e, 1.0) / (1.0 + e))

    row = lax.broadcasted_iota(jnp.int32, (c, c), 0)
    col = lax.broadcasted_iota(jnp.int32, (c, c), 1)
    visible = (col >= row) if reverse else (col <= row)
    tri_f = visible.astype(F32)
    log2_f = log_f * LOG2_E
    for ci in range(n_chunks):
        bsum = jnp.dot(tri_f, log2_f[ci * c:(ci + 1) * c], precision=lax.Precision.HIGHEST,
                       preferred_element_type=F32)
        b_ref[ci * per:(ci + 1) * per] = bsum.reshape(per, sub, kd)
    kk_ref[...] = kk.reshape(tl // sub, sub, kd)

    q3 = q_ref[...].reshape(tl // sub, sub, kd)
    b3 = b_ref[...]
    for s in range(sub):
        k_row = kk_ref[:, s:s + 1, :]
        b_row = b_ref[:, s:s + 1, :]
        w = q3 * k_row * jnp.exp2(jnp.minimum(b3 - b_row, 0.0))
        lhs_ref[:, s * kd:(s + 1) * kd] = w.reshape(tl, kd).astype(BF16)
    d_all = _dot(lhs_ref[...], sel_ref[...])

    q2 = q_ref[...]
    k2 = kk.reshape(tl, kd)
    b2 = b3.reshape(tl, kd)

    split_q, split_k, split_mask = [], [], []
    group = c
    while group > sub:
        half = group // 2
        pos = lax.broadcasted_iota(jnp.int32, (1, group, 1), 1)
        is_q = (pos < half) if reverse else (pos >= half)
        bg = b2.reshape(tl // group, group, kd)
        split_row = half if reverse else half - 1
        b_split = bg[:, split_row:split_row + 1, :]
        dec = jnp.exp2(jnp.where(is_q, bg - b_split, b_split - bg))
        split_q.append(jnp.where(is_q, q2.reshape(bg.shape) * dec, 0.0).astype(BF16).reshape(tl, kd))
        split_k.append(jnp.where(is_q, 0.0, k2.reshape(bg.shape) * dec).astype(BF16).reshape(tl, kd))
        split_mask.append(None if group == c else jnp.where((row // group) == (col // group), 1.0, 0.0))
        group = half
    mask_d = jnp.where(((row // sub) == (col // sub)) & visible, 1.0, 0.0)

    edge = 0 if reverse else c - 1
    bc3 = b2.reshape(n_chunks, c, kd)
    b_edge = bc3[:, edge:edge + 1, :]
    q_in = (q2.reshape(bc3.shape) * jnp.exp2(bc3)).astype(BF16)
    k_out = (k2.reshape(bc3.shape) * jnp.exp2(b_edge - bc3)).astype(BF16)
    edge_decay = jnp.exp2(b_edge)

    order = range(n_chunks - 1, -1, -1) if reverse else range(n_chunks)
    o_intra, s_add = {}, {}
    for ci in order:
        rows = slice(ci * c, (ci + 1) * c)
        att = d_all[rows, :] * mask_d
        for qs, ks, mask in zip(split_q, split_k, split_mask):
            part = _dot_nt(qs[rows, :], ks[rows, :])
            att = att + (part if mask is None else part * mask)
        vc = v_ref[rows, :]
        o_intra[ci] = _dot(att.astype(BF16), vc.astype(BF16))
        s_add[ci] = _dot(vc.T.astype(BF16), k_out[ci])

    st = s_ref[...]
    for ci in order:
        rows = slice(ci * c, (ci + 1) * c)
        o = o_intra[ci] + _dot_nt(q_in[ci], st.astype(BF16))
        st = st * edge_decay[ci] + s_add[ci]
        if reverse:
            o = _rms(o + of_ref[rows, :], ong_ref[...])
            g = gate_ref[rows, :]
            o_ref[rows, :] = (o * (g * jax.nn.sigmoid(g))).astype(o_ref.dtype)
        else:
            o_ref[rows, :] = o
    s_ref[...] = st


def hgrn_scan(proj, lb, reverse, tl, o_fwd=None, o_norm_g=None):
    b, l, _ = proj.shape
    h = HGRN_HEADS
    nl = l // tl
    n_chunks = tl // HGRN_CHUNK
    kd = HGRN_KEY_DIM

    def pos(j):
        return (nl - 1 - j) if reverse else j

    def col_spec(section):
        return pl.BlockSpec((None, tl, kd), lambda i, hh, j: (i, pos(j), section * h + hh))

    sel_s = lax.broadcasted_iota(jnp.int32, (HGRN_SUB, kd, HGRN_CHUNK), 0)
    sel_j = lax.broadcasted_iota(jnp.int32, (HGRN_SUB, kd, HGRN_CHUNK), 2)
    sel = (sel_j % HGRN_SUB == sel_s).astype(BF16).reshape(HGRN_SUB * kd, HGRN_CHUNK)

    in_specs = [col_spec(0), col_spec(2 if reverse else 1), col_spec(3),
                pl.BlockSpec((1, kd), lambda i, hh, j: (0, hh)), _resident(sel.shape)]
    args = [proj, proj, proj, lb, sel]
    out_spec = pl.BlockSpec((None, tl, kd), lambda i, hh, j: (i, pos(j), hh))
    if reverse:
        in_specs += [out_spec, col_spec(4), _resident((1, HGRN_VAL_DIM))]
        args += [o_fwd, proj, o_norm_g]
    return pl.pallas_call(
        functools.partial(_hgrn_scan_kernel, reverse=reverse, n_chunks=n_chunks),
        grid=(b, h, nl),
        in_specs=in_specs,
        out_specs=out_spec,
        out_shape=jax.ShapeDtypeStruct((b, l, D_MODEL), BF16 if reverse else F32),
        scratch_shapes=[
            pltpu.VMEM((HGRN_VAL_DIM, kd), F32),
            pltpu.VMEM((tl // HGRN_SUB, HGRN_SUB, kd), F32),
            pltpu.VMEM((tl // HGRN_SUB, HGRN_SUB, kd), F32),
            pltpu.VMEM((tl, HGRN_SUB * kd), BF16),
        ],
        compiler_params=_params("parallel", "parallel", "arbitrary"),
        name="hgrn_scan_bwd" if reverse else "hgrn_scan_fwd",
    )(*args)


def _fnet_channel_kernel(x_ref, g_ref, cn_ref, sn_ref, u_ref, w_ref):
    xn = _rms(x_ref[...], g_ref[...]).astype(BF16)
    for gi in range(FNET_GROUPS):
        cols = slice(gi * FNET_GROUP_DIM, (gi + 1) * FNET_GROUP_DIM)
        u_ref[:, cols] = _dot(xn[:, cols], cn_ref[...]).astype(BF16)
        w_ref[:, cols] = _dot(xn[:, cols], sn_ref[...]).astype(BF16)


def fnet_channel(x, g, cn, sn, tm):
    t, d = x.shape
    spec = pl.BlockSpec((tm, d), lambda i: (i, 0))
    return pl.pallas_call(
        _fnet_channel_kernel,
        grid=(t // tm,),
        in_specs=[spec, _resident((1, d)), _resident(cn.shape), _resident(sn.shape)],
        out_specs=[spec, spec],
        out_shape=[jax.ShapeDtypeStruct((t, d), BF16)] * 2,
        compiler_params=_params("parallel"),
        name="fnet_channel",
    )(x, g, cn, sn)


def _fnet_seq_kernel(cl_ref, nsl_ref, u_ref, w_ref, x_ref, wo_ref, o_ref, acc_ref):
    k = pl.program_id(2)

    @pl.when(k == 0)
    def _():
        acc_ref[...] = jnp.zeros_like(acc_ref)

    acc_ref[...] += _dot(cl_ref[...], u_ref[...]) + _dot(nsl_ref[...], w_ref[...])

    @pl.when(k == pl.num_programs(2) - 1)
    def _():
        o_ref[...] = x_ref[...] + _dot(acc_ref[...].astype(BF16), wo_ref[...])


def fnet_seq(cl, nsl, u, w, x, w_o, tm, tk):
    b, l, d = x.shape
    spec_dft = pl.BlockSpec((tm, tk), lambda i, j, k: (j, k))
    spec_uw = pl.BlockSpec((None, tk, d), lambda i, j, k: (i, k, 0))
    spec_x = pl.BlockSpec((None, tm, d), lambda i, j, k: (i, j, 0))
    return pl.pallas_call(
        _fnet_seq_kernel,
        grid=(b, l // tm, l // tk),
        in_specs=[spec_dft, spec_dft, spec_uw, spec_uw, spec_x, _resident(w_o.shape)],
        out_specs=spec_x,
        out_shape=jax.ShapeDtypeStruct((b, l, d), F32),
        scratch_shapes=[pltpu.VMEM((tm, d), F32)],
        compiler_params=_params("parallel", "parallel", "arbitrary"),
        name="fnet_seq",
    )(cl, nsl, u, w, x, w_o)


def _dft_tables(n):
    idx = jnp.arange(n, dtype=jnp.int32)
    jk = (idx[:, None] * idx[None, :]) % n
    ang = jk.astype(F32) * (2.0 * jnp.pi / n)
    scale = n ** -0.5
    return jnp.cos(ang) * scale, jnp.sin(ang) * scale


def _rope_tables(length):
    dim = QK_ROPE_DIM
    half = dim // 2
    inv = 1.0 / (ROPE_THETA ** (jnp.arange(0, dim, 2, dtype=F32) / dim))
    ang = jnp.arange(length, dtype=F32)[:, None] * inv[None, :]
    cos, sin = jnp.cos(ang), jnp.sin(ang)
    zero = jnp.zeros_like(cos)
    cos_t = jnp.concatenate([cos, cos, zero, zero], axis=-1)
    sin_lo = jnp.concatenate([-sin, zero, zero, zero], axis=-1)
    sin_hi = jnp.concatenate([zero, sin, zero, zero], axis=-1)
    return cos_t, sin_lo, sin_hi


def _pad_cols(a, n):
    return jnp.concatenate([a, jnp.zeros(a.shape[:-1] + (n,), a.dtype)], axis=-1)


def _row(v):
    return v.reshape(1, -1).astype(F32)


class _Tiles(NamedTuple):
    token: int
    mla_pre: int
    attn: int
    scan: int
    hgrn_cols: int
    ff: int


def _tiles(l):
    return _Tiles(token=min(l, 512), mla_pre=min(l, 256), attn=min(l, 1024), scan=min(l, 1024),
                  hgrn_cols=2048, ff=1024)


def kernel(x_prompt, x_sample, mem_prompt, mem_sample, norm_mix_g, mla_w_in, mla_q_a_norm_g, mla_kv_a_norm_g, mla_w_q_b, mla_w_kv_b, mla_q_norm_g, mla_k_norm_g, mla_w_o, hgrn_w_in, hgrn_lb_logits, hgrn_o_norm_g, hgrn_w_o, fnet_w_o, norm_xq_g, norm_mem_g, mem_w_q, mem_w_kv, mem_q_norm_g, mem_k_norm_g, mem_w_o, norm_mlp_g, mlp_w1, mlp_w2):
    depth = norm_mix_g.shape[0]
    n_mixers = 3

    lb_all = jnp.cumsum(jax.nn.softmax(hgrn_lb_logits.astype(F32), axis=0), axis=0)
    lb_all = lb_all - lb_all[:1]

    mla_w_in_p = _pad_cols(mla_w_in, LANE - QK_ROPE_DIM).astype(BF16)
    n_mla = mla_w_in.shape[0]
    wqb = mla_w_q_b.reshape(n_mla, Q_LORA_RANK, MLA_HEADS, QK_HEAD_DIM)
    mla_w_qb_p = _pad_cols(wqb, QK_PAD_DIM - QK_HEAD_DIM).reshape(n_mla, Q_LORA_RANK, MLA_HEADS * QK_PAD_DIM).astype(BF16)
    mla_w_kvb = mla_w_kv_b.astype(BF16)
    mla_gq_p = _pad_cols(mla_q_norm_g * (QK_HEAD_DIM ** -0.5 * LOG2_E), QK_PAD_DIM - QK_HEAD_DIM)
    mla_score_bound = (QK_HEAD_DIM ** 0.5 * LOG2_E * 1.02) * (
        jnp.max(jnp.abs(mla_q_norm_g), axis=-1) * jnp.max(jnp.abs(mla_k_norm_g), axis=-1))
    mla_gk_p = _pad_cols(mla_k_norm_g, QK_PAD_DIM - QK_HEAD_DIM)
    mla_w_o_b = mla_w_o.astype(BF16)
    hgrn_w_in_b = hgrn_w_in.astype(BF16)
    hgrn_w_o_b = hgrn_w_o.astype(BF16)
    fnet_w_o_b = fnet_w_o.astype(BF16)
    mem_w_q_b = mem_w_q.astype(BF16)
    mem_w_kv_b = mem_w_kv.astype(BF16)
    mem_w_o_b = mem_w_o.astype(BF16)
    mem_qg_scaled = mem_q_norm_g * (MEM_HEAD_DIM ** -0.5)
    mlp_w1_b = mlp_w1.astype(BF16)
    mlp_w2_b = mlp_w2.astype(BF16)

    has_fnet = depth >= n_mixers
    if has_fnet:
        cn, sn = _dft_tables(FNET_GROUP_DIM)
        cn, sn = cn.astype(BF16), sn.astype(BF16)

    def run_trunk(x, mem):
        b, l, d = x.shape
        t = b * l
        tiles = _tiles(l)
        tm = tiles.token
        if depth >= 1:
            rope_tabs = _rope_tables(l)
        if has_fnet:
            cl, sl = _dft_tables(l)
            cl, nsl = cl.astype(BF16), (-sl).astype(BF16)
        for i in range(depth):
            kind = i % n_mixers
            j = i // n_mixers
            gmix = _row(norm_mix_g[i])
            if kind == 0:
                q, k, v = mla_pre(x, gmix, mla_w_in_p[j], _row(mla_q_a_norm_g[j]), _row(mla_kv_a_norm_g[j]),
                                  mla_w_qb_p[j], mla_w_kvb[j], _row(mla_gq_p[j]), _row(mla_gk_p[j]),
                                  *rope_tabs, tm=tiles.mla_pre)
                o = lax.cond(mla_score_bound[j] <= SCORE_LOG2_LIMIT,
                             functools.partial(flash_attention_bounded, tq=tiles.attn, tk=tiles.attn),
                             functools.partial(flash_attention, tq=tiles.attn, tk=tiles.attn), q, k, v)
                x = matmul_residual(o.reshape(t, d), mla_w_o_b[j], x.reshape(t, d), tm).reshape(b, l, d)
            elif kind == 1:
                proj = norm_matmul(x.reshape(t, d), gmix, hgrn_w_in_b[j], F32, tm, tiles.hgrn_cols).reshape(b, l, -1)
                tl = tiles.scan
                o_f = hgrn_scan(proj, _row(lb_all[i, 0]), False, tl)
                y = hgrn_scan(proj, _row(lb_all[i, 1]), True, tl, o_fwd=o_f, o_norm_g=_row(hgrn_o_norm_g[j]))
                x = matmul_residual(y.reshape(t, d), hgrn_w_o_b[j], x.reshape(t, d), tm).reshape(b, l, d)
            else:
                u, w = fnet_channel(x.reshape(t, d), gmix, cn, sn, tm)
                x = fnet_seq(cl, nsl, u.reshape(b, l, d), w.reshape(b, l, d), x, fnet_w_o_b[j], tm, tm)
            mk, mv = mem_kv(mem, _row(norm_mem_g[i]), mem_w_kv_b[i], _row(mem_k_norm_g[i]))
            x = mem_xattn(x, _row(norm_xq_g[i]), mem_w_q_b[i], _row(mem_qg_scaled[i]), mk, mv, mem_w_o_b[i], tm)
            x = mlp(x.reshape(t, d), _row(norm_mlp_g[i]), mlp_w1_b[i], mlp_w2_b[i], tm, tiles.ff).reshape(b, l, d)
        return x

    return (run_trunk(x_prompt, mem_prompt), run_trunk(x_sample, mem_sample))
```

```python
import functools
from typing import NamedTuple

import jax
import jax.numpy as jnp
from jax import lax
from jax.experimental import pallas as pl
from jax.experimental.pallas import tpu as pltpu

F32 = jnp.float32
BF16 = jnp.bfloat16

EPS = 1e-6
D_MODEL = 2048
D_FF = 4 * D_MODEL

MLA_HEADS = 16
Q_LORA_RANK = 512
KV_LORA_RANK = 512
QK_NOPE_DIM = 128
QK_ROPE_DIM = 64
QK_HEAD_DIM = QK_NOPE_DIM + QK_ROPE_DIM
V_HEAD_DIM = 128
ROPE_THETA = 10000.0
QK_PAD_DIM = 256
SCORE_LOG2_LIMIT = 100.0
ONES_ROWS = 16

HGRN_HEADS = 16
HGRN_KEY_DIM = 128
HGRN_VAL_DIM = D_MODEL // HGRN_HEADS
HGRN_KEY_WIDTH = HGRN_HEADS * HGRN_KEY_DIM
HGRN_CHUNK = 64
HGRN_SUB = 8
LOG2_E = 1.4426950408889634

FNET_GROUPS = 4
FNET_GROUP_DIM = D_MODEL // FNET_GROUPS

MEM_HEADS = 4
MEM_HEAD_DIM = 128
MEM_WIDTH = MEM_HEADS * MEM_HEAD_DIM

LANE = 128
VMEM_LIMIT_BYTES = 56 * 1024 * 1024


def _params(*semantics):
    return pltpu.CompilerParams(dimension_semantics=semantics, vmem_limit_bytes=VMEM_LIMIT_BYTES)


def _resident(shape):
    nd = len(shape)
    return pl.BlockSpec(shape, lambda *_: (0,) * nd, pipeline_mode=pl.Buffered(1))


def _rms(x, g):
    ms = jnp.mean(x * x, axis=-1, keepdims=True)
    return x * lax.rsqrt(ms + EPS) * g


def _dot(a, b):
    return jnp.dot(a, b, preferred_element_type=F32)


def _dot_nt(a, b):
    return lax.dot_general(a, b, (((1,), (1,)), ((), ())), preferred_element_type=F32)


def _norm_matmul_kernel(x_ref, g_ref, w_ref, o_ref, xn_ref):
    @pl.when(pl.program_id(1) == 0)
    def _():
        xn_ref[...] = _rms(x_ref[...], g_ref[...]).astype(BF16)

    o_ref[...] = _dot(xn_ref[...], w_ref[...]).astype(o_ref.dtype)


def norm_matmul(x, g, w, out_dtype, tm, tn):
    t, d = x.shape
    n = w.shape[1]
    return pl.pallas_call(
        _norm_matmul_kernel,
        grid=(t // tm, n // tn),
        in_specs=[
            pl.BlockSpec((tm, d), lambda i, j: (i, 0)),
            _resident((1, d)),
            pl.BlockSpec((d, tn), lambda i, j: (0, j)),
        ],
        out_specs=pl.BlockSpec((tm, tn), lambda i, j: (i, j)),
        out_shape=jax.ShapeDtypeStruct((t, n), out_dtype),
        scratch_shapes=[pltpu.VMEM((tm, d), BF16)],
        compiler_params=_params("parallel", "arbitrary"),
        name="norm_matmul",
    )(x, g, w)


def _matmul_residual_kernel(a_ref, w_ref, r_ref, o_ref):
    o_ref[...] = r_ref[...] + _dot(a_ref[...], w_ref[...])


def matmul_residual(a, w, res, tm):
    t, k = a.shape
    n = w.shape[1]
    return pl.pallas_call(
        _matmul_residual_kernel,
        grid=(t // tm,),
        in_specs=[
            pl.BlockSpec((tm, k), lambda i: (i, 0)),
            _resident((k, n)),
            pl.BlockSpec((tm, n), lambda i: (i, 0)),
        ],
        out_specs=pl.BlockSpec((tm, n), lambda i: (i, 0)),
        out_shape=jax.ShapeDtypeStruct((t, n), F32),
        compiler_params=_params("parallel"),
        name="matmul_residual",
    )(a, w, res)


def _mlp_kernel(x_ref, g_ref, w1_ref, w2_ref, o_ref, xn_ref, acc_ref):
    j = pl.program_id(1)

    @pl.when(j == 0)
    def _():
        xn_ref[...] = _rms(x_ref[...], g_ref[...]).astype(BF16)
        acc_ref[...] = jnp.zeros_like(acc_ref)

    h = jnp.maximum(_dot(xn_ref[...], w1_ref[...]), 0.0)
    acc_ref[...] += _dot((h * h).astype(BF16), w2_ref[...])

    @pl.when(j == pl.num_programs(1) - 1)
    def _():
        o_ref[...] = x_ref[...] + acc_ref[...]


def mlp(x, g, w1, w2, tm, tf):
    t, d = x.shape
    f = w1.shape[1]
    return pl.pallas_call(
        _mlp_kernel,
        grid=(t // tm, f // tf),
        in_specs=[
            pl.BlockSpec((tm, d), lambda i, j: (i, 0)),
            _resident((1, d)),
            pl.BlockSpec((d, tf), lambda i, j: (0, j)),
            pl.BlockSpec((tf, d), lambda i, j: (j, 0)),
        ],
        out_specs=pl.BlockSpec((tm, d), lambda i, j: (i, 0)),
        out_shape=jax.ShapeDtypeStruct((t, d), F32),
        scratch_shapes=[pltpu.VMEM((tm, d), BF16), pltpu.VMEM((tm, d), F32)],
        compiler_params=_params("parallel", "arbitrary"),
        name="mlp",
    )(x, g, w1, w2)


def _mem_kv_kernel(m_ref, g_ref, w_ref, kg_ref, k_out, v_out):
    mn = _rms(m_ref[...], g_ref[...]).astype(BF16)
    kv = _dot(mn, w_ref[...])
    kg = kg_ref[...]
    for h in range(MEM_HEADS):
        lo = h * MEM_HEAD_DIM
        k_out[:, lo:lo + MEM_HEAD_DIM] = _rms(kv[:, lo:lo + MEM_HEAD_DIM], kg).astype(BF16)
    v_out[...] = kv[:, MEM_WIDTH:].astype(BF16)


def mem_kv(mem, g, w_kv, k_g):
    b, m, d = mem.shape
    spec_out = pl.BlockSpec((None, m, MEM_WIDTH), lambda i: (i, 0, 0))
    return pl.pallas_call(
        _mem_kv_kernel,
        grid=(b,),
        in_specs=[
            pl.BlockSpec((None, m, d), lambda i: (i, 0, 0)),
            _resident((1, d)),
            _resident((d, 2 * MEM_WIDTH)),
            _resident((1, MEM_HEAD_DIM)),
        ],
        out_specs=[spec_out, spec_out],
        out_shape=[jax.ShapeDtypeStruct((b, m, MEM_WIDTH), BF16)] * 2,
        compiler_params=_params("parallel"),
        name="mem_kv",
    )(mem, g, w_kv, k_g)


def _mem_xattn_kernel(x_ref, g_ref, wq_ref, qg_ref, k_ref, v_ref, wo_ref, o_ref):
    x = x_ref[...]
    q = _dot(_rms(x, g_ref[...]).astype(BF16), wq_ref[...])
    qg = qg_ref[...]
    heads = []
    for h in range(MEM_HEADS):
        lo = h * MEM_HEAD_DIM
        qh = _rms(q[:, lo:lo + MEM_HEAD_DIM], qg).astype(BF16)
        s = _dot_nt(qh, k_ref[:, lo:lo + MEM_HEAD_DIM])
        e = jnp.exp(s - jnp.max(s, axis=-1, keepdims=True))
        p = e / jnp.sum(e, axis=-1, keepdims=True)
        heads.append(_dot(p.astype(BF16), v_ref[:, lo:lo + MEM_HEAD_DIM]))
    o = jnp.concatenate(heads, axis=-1).astype(BF16)
    o_ref[...] = x + _dot(o, wo_ref[...])


def mem_xattn(x, g, w_q, q_g_scaled, k, v, w_o, tm):
    b, l, d = x.shape
    m = k.shape[1]
    spec_x = pl.BlockSpec((None, tm, d), lambda i, j: (i, j, 0))
    spec_kv = pl.BlockSpec((None, m, MEM_WIDTH), lambda i, j: (i, 0, 0))
    return pl.pallas_call(
        _mem_xattn_kernel,
        grid=(b, l // tm),
        in_specs=[
            spec_x,
            _resident((1, d)),
            _resident((d, MEM_WIDTH)),
            _resident((1, MEM_HEAD_DIM)),
            spec_kv,
            spec_kv,
            _resident((MEM_WIDTH, d)),
        ],
        out_specs=spec_x,
        out_shape=jax.ShapeDtypeStruct((b, l, d), F32),
        compiler_params=_params("parallel", "parallel"),
        name="mem_xattn",
    )(x, g, w_q, q_g_scaled, k, v, w_o)


def _mla_pre_kernel(x_ref, gmix_ref, win_ref, qag_ref, kvag_ref, wqt_ref, wk_ref, wvt_ref, gqt_ref, gk_ref,
                    cosk_ref, sin_lo_ref, sin_hi_ref, cosq_ref, sinq_ref, q_out, k_out, v_out):
    tm = x_ref.shape[0]
    half = QK_ROPE_DIM // 2
    xn = _rms(x_ref[...], gmix_ref[...]).astype(BF16)
    proj = _dot(xn, win_ref[...])
    ql = _rms(proj[:, :Q_LORA_RANK], qag_ref[...])
    kvl = _rms(proj[:, Q_LORA_RANK:Q_LORA_RANK + KV_LORA_RANK], kvag_ref[...])
    kr = proj[:, Q_LORA_RANK + KV_LORA_RANK:]
    kvl_t = kvl.T.astype(BF16)
    qt = _dot(wqt_ref[...], ql.T.astype(BF16))
    vt = _dot(wvt_ref[...], kvl_t)
    kn_all = _dot(kvl.astype(BF16), wk_ref[...])
    inv_dim = 1.0 / QK_HEAD_DIM

    gk = gk_ref[...]
    ss_kr = jnp.sum(kr * kr, axis=-1, keepdims=True)
    krg = kr * gk[:, QK_NOPE_DIM:]
    kr_rot = (krg * cosk_ref[...] + pltpu.roll(krg, half, 1) * sin_hi_ref[...]
              + pltpu.roll(krg, LANE - half, 1) * sin_lo_ref[...])
    reps = tm // LANE
    gqt = jnp.tile(gqt_ref[...], (1, reps))
    cos_q = cosq_ref[...]
    sin_q = sinq_ref[...]
    zero_rows = jnp.zeros((QK_PAD_DIM - QK_HEAD_DIM, tm), BF16)
    for h in range(MLA_HEADS):
        qh = qt[h * QK_PAD_DIM:h * QK_PAD_DIM + QK_HEAD_DIM, :]
        inv_q = lax.rsqrt(jnp.sum(qh * qh, axis=0, keepdims=True) * inv_dim + EPS)
        qn = qh * inv_q * gqt
        x1 = qn[QK_NOPE_DIM:QK_NOPE_DIM + half, :]
        x2 = qn[QK_NOPE_DIM + half:, :]
        q_out[h, :QK_NOPE_DIM, :] = qn[:QK_NOPE_DIM, :].astype(BF16)
        q_out[h, QK_NOPE_DIM:QK_NOPE_DIM + half, :] = (x1 * cos_q - x2 * sin_q).astype(BF16)
        q_out[h, QK_NOPE_DIM + half:QK_HEAD_DIM, :] = (x1 * sin_q + x2 * cos_q).astype(BF16)
        q_out[h, QK_HEAD_DIM:, :] = zero_rows
        kn = kn_all[:, h * QK_NOPE_DIM:(h + 1) * QK_NOPE_DIM]
        inv_k = lax.rsqrt((jnp.sum(kn * kn, axis=-1, keepdims=True) + ss_kr) * inv_dim + EPS)
        k_out[h, :, :QK_NOPE_DIM] = (kn * inv_k * gk[:, :QK_NOPE_DIM]).astype(BF16)
        k_out[h, :, QK_NOPE_DIM:] = (kr_rot * inv_k).astype(BF16)
        v_out[h] = vt[h * V_HEAD_DIM:(h + 1) * V_HEAD_DIM, :].astype(BF16)


def mla_pre(x, gmix, w_in_p, qag, kvag, w_qt, w_k, w_vt, gq_t, gk_p, rope_tabs, tm):
    b, l, d = x.shape
    h = MLA_HEADS
    cos_k, sin_lo, sin_hi, cos_q, sin_q = rope_tabs
    spec_tab_k = pl.BlockSpec((tm, LANE), lambda i, j: (j, 0))
    spec_tab_q = pl.BlockSpec((QK_ROPE_DIM // 2, tm), lambda i, j: (0, j))
    spec_qv = lambda rows: pl.BlockSpec((None, h, rows, tm), lambda i, j: (i, 0, 0, j))
    return pl.pallas_call(
        _mla_pre_kernel,
        grid=(b, l // tm),
        in_specs=[
            pl.BlockSpec((None, tm, d), lambda i, j: (i, j, 0)),
            _resident((1, d)),
            _resident(w_in_p.shape),
            _resident((1, Q_LORA_RANK)),
            _resident((1, KV_LORA_RANK)),
            _resident(w_qt.shape),
            _resident(w_k.shape),
            _resident(w_vt.shape),
            _resident(gq_t.shape),
            _resident((1, QK_PAD_DIM)),
            spec_tab_k, spec_tab_k, spec_tab_k, spec_tab_q, spec_tab_q,
        ],
        out_specs=[spec_qv(QK_PAD_DIM), pl.BlockSpec((None, h, tm, QK_PAD_DIM), lambda i, j: (i, 0, j, 0)),
                   spec_qv(V_HEAD_DIM)],
        out_shape=[
            jax.ShapeDtypeStruct((b, h, QK_PAD_DIM, l), BF16),
            jax.ShapeDtypeStruct((b, h, l, QK_PAD_DIM), BF16),
            jax.ShapeDtypeStruct((b, h, V_HEAD_DIM, l), BF16),
        ],
        compiler_params=_params("parallel", "parallel"),
        name="mla_pre",
    )(x, gmix, w_in_p, qag, kvag, w_qt, w_k, w_vt, gq_t, gk_p, cos_k, sin_lo, sin_hi, cos_q, sin_q)


def _flash_kernel(q_ref, k_ref, vt_ref, o_ref, m_ref, l_ref, acc_ref):
    kv = pl.program_id(3)

    @pl.when(kv == 0)
    def _():
        m_ref[...] = jnp.full_like(m_ref, -jnp.inf)
        l_ref[...] = jnp.zeros_like(l_ref)
        acc_ref[...] = jnp.zeros_like(acc_ref)

    s = _dot(k_ref[...], q_ref[...])
    m_prev = m_ref[...]
    m_new = jnp.maximum(m_prev, jnp.max(s, axis=0, keepdims=True))
    alpha = jnp.exp2(m_prev - m_new)
    p = jnp.exp2(s - m_new)
    l_ref[...] = alpha * l_ref[...] + jnp.sum(p, axis=0, keepdims=True)
    acc_ref[...] = alpha * acc_ref[...] + _dot(vt_ref[...], p.astype(BF16))
    m_ref[...] = m_new

    @pl.when(kv == pl.num_programs(3) - 1)
    def _():
        o_ref[...] = (acc_ref[...] / l_ref[...]).T.astype(o_ref.dtype)


def flash_attention(qt, k, vt, tq, tk):
    b, h, dk, l = qt.shape
    dv = vt.shape[2]
    return pl.pallas_call(
        _flash_kernel,
        grid=(b, h, l // tq, l // tk),
        in_specs=[
            pl.BlockSpec((None, None, dk, tq), lambda i, j, a, c: (i, j, 0, a)),
            pl.BlockSpec((None, None, tk, dk), lambda i, j, a, c: (i, j, c, 0)),
            pl.BlockSpec((None, None, dv, tk), lambda i, j, a, c: (i, j, 0, c)),
        ],
        out_specs=pl.BlockSpec((None, tq, dv), lambda i, j, a, c: (i, a, j)),
        out_shape=jax.ShapeDtypeStruct((b, l, h * dv), BF16),
        scratch_shapes=[pltpu.VMEM((1, tq), F32), pltpu.VMEM((1, tq), F32), pltpu.VMEM((dv, tq), F32)],
        compiler_params=_params("parallel", "parallel", "parallel", "arbitrary"),
        name="flash_attention",
    )(qt, k, vt)


def _flash_bounded_kernel(q_ref, k_ref, vt_ref, o_ref, acc_ref, *, tk):
    dv = vt_ref.shape[0]
    q = q_ref[...]
    ones = jnp.ones((ONES_ROWS, tk), BF16)
    for c in range(k_ref.shape[0] // tk):
        cols = slice(c * tk, (c + 1) * tk)
        p = jnp.exp2(_dot(k_ref[cols, :], q)).astype(BF16)
        pv = _dot(jnp.concatenate([vt_ref[:, cols], ones], axis=0), p)
        if c == 0:
            acc_ref[...] = pv
        else:
            acc_ref[...] += pv
    acc = acc_ref[...]
    o_ref[...] = (acc[:dv] / acc[dv:dv + 1]).T.astype(o_ref.dtype)


def flash_attention_bounded(qt, k, vt, tq, tk):
    b, h, dk, l = qt.shape
    dv = vt.shape[2]
    return pl.pallas_call(
        functools.partial(_flash_bounded_kernel, tk=tk),
        grid=(b, h, l // tq),
        in_specs=[
            pl.BlockSpec((None, None, dk, tq), lambda i, j, a: (i, j, 0, a)),
            pl.BlockSpec((None, None, l, dk), lambda i, j, a: (i, j, 0, 0)),
            pl.BlockSpec((None, None, dv, l), lambda i, j, a: (i, j, 0, 0)),
        ],
        out_specs=pl.BlockSpec((None, tq, dv), lambda i, j, a: (i, a, j)),
        out_shape=jax.ShapeDtypeStruct((b, l, h * dv), BF16),
        scratch_shapes=[pltpu.VMEM((dv + ONES_ROWS, tq), F32)],
        compiler_params=_params("parallel", "parallel", "arbitrary"),
        name="flash_attention_bounded",
    )(qt, k, vt)


def _hgrn_scan_kernel(*refs, reverse, n_chunks):
    if reverse:
        q_ref, z_ref, v_ref, lb_ref, sel_ref, of_ref, gate_ref, ong_ref, o_ref, s_ref, b_ref, kk_ref, lhs_ref = refs
    else:
        q_ref, z_ref, v_ref, lb_ref, sel_ref, o_ref, s_ref, b_ref, kk_ref, lhs_ref = refs
    c = HGRN_CHUNK
    sub = HGRN_SUB
    per = c // sub
    kd = HGRN_KEY_DIM
    tl = n_chunks * c

    @pl.when(pl.program_id(2) == 0)
    def _():
        s_ref[...] = jnp.zeros_like(s_ref)

    lb = lb_ref[...]
    z = z_ref[...]
    e = jnp.exp(-jnp.abs(z))
    log_sig = jnp.minimum(z, 0.0) - jnp.log(1.0 + e)
    a = jnp.log(lb)
    cterm = jnp.log1p(-lb) + log_sig
    log_f = jnp.maximum(a, cterm) + jnp.log(1.0 + jnp.exp(-jnp.abs(a - cterm)))
    kk = (1.0 - lb) * (jnp.where(z >= 0.0, e, 1.0) / (1.0 + e))

    row = lax.broadcasted_iota(jnp.int32, (c, c), 0)
    col = lax.broadcasted_iota(jnp.int32, (c, c), 1)
    visible = (col >= row) if reverse else (col <= row)
    tri_f = visible.astype(F32)
    log2_f = log_f * LOG2_E
    for ci in range(n_chunks):
        bsum = jnp.dot(tri_f, log2_f[ci * c:(ci + 1) * c], precision=lax.Precision.HIGHEST,
                       preferred_element_type=F32)
        b_ref[ci * per:(ci + 1) * per] = bsum.reshape(per, sub, kd)
    kk_ref[...] = kk.reshape(tl // sub, sub, kd)

    q3 = q_ref[...].reshape(tl // sub, sub, kd)
    b3 = b_ref[...]
    for s in range(sub):
        k_row = kk_ref[:, s:s + 1, :]
        b_row = b_ref[:, s:s + 1, :]
        w = q3 * k_row * jnp.exp2(jnp.minimum(b3 - b_row, 0.0))
        lhs_ref[:, s * kd:(s + 1) * kd] = w.reshape(tl, kd).astype(BF16)
    d_all = _dot(lhs_ref[...], sel_ref[...])

    q2 = q_ref[...]
    k2 = kk.reshape(tl, kd)
    b2 = b3.reshape(tl, kd)

    split_q, split_k, split_mask = [], [], []
    group = c
    while group > sub:
        half = group // 2
        pos = lax.broadcasted_iota(jnp.int32, (1, group, 1), 1)
        is_q = (pos < half) if reverse else (pos >= half)
        bg = b2.reshape(tl // group, group, kd)
        split_row = half if reverse else half - 1
        b_split = bg[:, split_row:split_row + 1, :]
        dec = jnp.exp2(jnp.where(is_q, bg - b_split, b_split - bg))
        split_q.append(jnp.where(is_q, q2.reshape(bg.shape) * dec, 0.0).astype(BF16).reshape(tl, kd))
        split_k.append(jnp.where(is_q, 0.0, k2.reshape(bg.shape) * dec).astype(BF16).reshape(tl, kd))
        split_mask.append(None if group == c else jnp.where((row // group) == (col // group), 1.0, 0.0))
        group = half
    mask_d = jnp.where(((row // sub) == (col // sub)) & visible, 1.0, 0.0)

    edge = 0 if reverse else c - 1
    bc3 = b2.reshape(n_chunks, c, kd)
    b_edge = bc3[:, edge:edge + 1, :]
    q_in = (q2.reshape(bc3.shape) * jnp.exp2(bc3)).astype(BF16)
    k_out = (k2.reshape(bc3.shape) * jnp.exp2(b_edge - bc3)).astype(BF16)
    edge_decay = jnp.exp2(b_edge)

    order = range(n_chunks - 1, -1, -1) if reverse else range(n_chunks)
    o_intra, s_add = {}, {}
    for ci in order:
        rows = slice(ci * c, (ci + 1) * c)
        att = d_all[rows, :] * mask_d
        for qs, ks, mask in zip(split_q, split_k, split_mask):
            part = _dot_nt(qs[rows, :], ks[rows, :])
            att = att + (part if mask is None else part * mask)
        vc = v_ref[rows, :]
        o_intra[ci] = _dot(att.astype(BF16), vc.astype(BF16))
        s_add[ci] = _dot(vc.T.astype(BF16), k_out[ci])

    st = s_ref[...]
    for ci in order:
        rows = slice(ci * c, (ci + 1) * c)
        o = o_intra[ci] + _dot_nt(q_in[ci], st.astype(BF16))
        st = st * edge_decay[ci] + s_add[ci]
        if reverse:
            o = _rms(o + of_ref[rows, :], ong_ref[...])
            g = gate_ref[rows, :]
            o_ref[rows, :] = (o * (g * jax.nn.sigmoid(g))).astype(o_ref.dtype)
        else:
            o_ref[rows, :] = o
    s_ref[...] = st


def hgrn_scan(proj, lb, reverse, tl, o_fwd=None, o_norm_g=None):
    b, l, _ = proj.shape
    h = HGRN_HEADS
    nl = l // tl
    n_chunks = tl // HGRN_CHUNK
    kd = HGRN_KEY_DIM

    def pos(j):
        return (nl - 1 - j) if reverse else j

    def col_spec(section):
        return pl.BlockSpec((None, tl, kd), lambda i, hh, j: (i, pos(j), section * h + hh))

    sel_s = lax.broadcasted_iota(jnp.int32, (HGRN_SUB, kd, HGRN_CHUNK), 0)
    sel_j = lax.broadcasted_iota(jnp.int32, (HGRN_SUB, kd, HGRN_CHUNK), 2)
    sel = (sel_j % HGRN_SUB == sel_s).astype(BF16).reshape(HGRN_SUB * kd, HGRN_CHUNK)

    in_specs = [col_spec(0), col_spec(2 if reverse else 1), col_spec(3),
                pl.BlockSpec((1, kd), lambda i, hh, j: (0, hh)), _resident(sel.shape)]
    args = [proj, proj, proj, lb, sel]
    out_spec = pl.BlockSpec((None, tl, kd), lambda i, hh, j: (i, pos(j), hh))
    if reverse:
        in_specs += [out_spec, col_spec(4), _resident((1, HGRN_VAL_DIM))]
        args += [o_fwd, proj, o_norm_g]
    return pl.pallas_call(
        functools.partial(_hgrn_scan_kernel, reverse=reverse, n_chunks=n_chunks),
        grid=(b, h, nl),
        in_specs=in_specs,
        out_specs=out_spec,
        out_shape=jax.ShapeDtypeStruct((b, l, D_MODEL), BF16 if reverse else F32),
        scratch_shapes=[
            pltpu.VMEM((HGRN_VAL_DIM, kd), F32),
            pltpu.VMEM((tl // HGRN_SUB, HGRN_SUB, kd), F32),
            pltpu.VMEM((tl // HGRN_SUB, HGRN_SUB, kd), F32),
            pltpu.VMEM((tl, HGRN_SUB * kd), BF16),
        ],
        compiler_params=_params("parallel", "parallel", "arbitrary"),
        name="hgrn_scan_bwd" if reverse else "hgrn_scan_fwd",
    )(*args)


def _fnet_channel_kernel(x_ref, g_ref, cn_ref, sn_ref, u_ref, w_ref):
    xn = _rms(x_ref[...], g_ref[...]).astype(BF16)
    for gi in range(FNET_GROUPS):
        cols = slice(gi * FNET_GROUP_DIM, (gi + 1) * FNET_GROUP_DIM)
        u_ref[:, cols] = _dot(xn[:, cols], cn_ref[...]).astype(BF16)
        w_ref[:, cols] = _dot(xn[:, cols], sn_ref[...]).astype(BF16)


def fnet_channel(x, g, cn, sn, tm):
    t, d = x.shape
    spec = pl.BlockSpec((tm, d), lambda i: (i, 0))
    return pl.pallas_call(
        _fnet_channel_kernel,
        grid=(t // tm,),
        in_specs=[spec, _resident((1, d)), _resident(cn.shape), _resident(sn.shape)],
        out_specs=[spec, spec],
        out_shape=[jax.ShapeDtypeStruct((t, d), BF16)] * 2,
        compiler_params=_params("parallel"),
        name="fnet_channel",
    )(x, g, cn, sn)


def _fnet_stage1_kernel(m1_ref, u_ref, w_ref, are_ref, aim_ref, *, n_slabs):
    l1 = u_ref.shape[0]
    width = u_ref.shape[1] // n_slabs
    m1 = m1_ref[...]
    for j in range(n_slabs):
        cols = slice(j * width, (j + 1) * width)
        a = _dot(m1, jnp.concatenate([u_ref[:, cols], w_ref[:, cols]], axis=0))
        are_ref[:, cols] = a[:l1].astype(BF16)
        aim_ref[:, cols] = a[l1:].astype(BF16)


def fnet_stage1(m1, u, w, l1, l2, n_slabs):
    b, l, d = u.shape
    u_v, w_v = u.reshape(b, l1, l2 * d), w.reshape(b, l1, l2 * d)
    spec = pl.BlockSpec((None, l1, n_slabs * d), lambda i, j: (i, 0, j))
    a_re, a_im = pl.pallas_call(
        functools.partial(_fnet_stage1_kernel, n_slabs=n_slabs),
        grid=(b, l2 // n_slabs),
        in_specs=[_resident(m1.shape), spec, spec],
        out_specs=[spec, spec],
        out_shape=[jax.ShapeDtypeStruct((b, l1, l2 * d), BF16)] * 2,
        compiler_params=_params("parallel", "parallel"),
        name="fnet_stage1",
    )(m1, u_v, w_v)
    return a_re.reshape(b, l, d), a_im.reshape(b, l, d)


def _fnet_stage2_kernel(are_ref, aim_ref, twc_ref, tws_ref, m2_ref, x_ref, wo_ref, o_ref, y_ref, *, n_slabs):
    d = are_ref.shape[1]
    l2 = are_ref.shape[0] // n_slabs
    m2 = m2_ref[...]
    for j in range(n_slabs):
        rows = slice(j * l2, (j + 1) * l2)
        a_re = are_ref[rows, :].astype(F32)
        a_im = aim_ref[rows, :].astype(F32)
        c = jnp.tile(twc_ref[rows, :], (1, d // LANE))
        s = jnp.tile(tws_ref[rows, :], (1, d // LANE))
        b_re = a_re * c + a_im * s
        b_im = a_im * c - a_re * s
        y = _dot(m2, jnp.concatenate([b_re, b_im], axis=0).astype(BF16))
        y_ref[rows, :] = y.astype(BF16)
    out = _dot(y_ref[...], wo_ref[...])
    for j in range(n_slabs):
        cols = slice(j * d, (j + 1) * d)
        o_ref[:, cols] = x_ref[:, cols] + out[j * l2:(j + 1) * l2, :]


def fnet_stage2(a_re, a_im, tw_c, tw_s, m2, x, w_o, l1, l2, n_slabs):
    b, l, d = x.shape
    x_v = x.reshape(b, l2, l1 * d)
    spec_a = pl.BlockSpec((None, n_slabs * l2, d), lambda i, j: (i, j, 0))
    spec_tw = pl.BlockSpec((n_slabs * l2, LANE), lambda i, j: (j, 0))
    spec_x = pl.BlockSpec((None, l2, n_slabs * d), lambda i, j: (i, 0, j))
    out = pl.pallas_call(
        functools.partial(_fnet_stage2_kernel, n_slabs=n_slabs),
        grid=(b, l1 // n_slabs),
        in_specs=[spec_a, spec_a, spec_tw, spec_tw, _resident(m2.shape), spec_x, _resident(w_o.shape)],
        out_specs=spec_x,
        out_shape=jax.ShapeDtypeStruct((b, l2, l1 * d), F32),
        scratch_shapes=[pltpu.VMEM((n_slabs * l2, d), BF16)],
        compiler_params=_params("parallel", "parallel"),
        name="fnet_stage2",
    )(a_re, a_im, tw_c, tw_s, m2, x_v, w_o)
    return out.reshape(b, l, d)


def _fnet_seq_tables(l1, l2):
    c1, s1 = _dft_tables(l1)
    c2, s2 = _dft_tables(l2)
    m1 = jnp.concatenate([jnp.concatenate([c1, -s1], axis=1), jnp.concatenate([-s1, -c1], axis=1)], axis=0)
    m2 = jnp.concatenate([c2, s2], axis=1)
    k1 = jnp.arange(l1, dtype=jnp.int32)[:, None]
    n2 = jnp.arange(l2, dtype=jnp.int32)[None, :]
    ang = ((k1 * n2) % (l1 * l2)).astype(F32).reshape(l1 * l2, 1) * (2.0 * jnp.pi / (l1 * l2))
    tw_c = jnp.broadcast_to(jnp.cos(ang), (l1 * l2, LANE))
    tw_s = jnp.broadcast_to(jnp.sin(ang), (l1 * l2, LANE))
    return m1.astype(BF16), m2.astype(BF16), tw_c, tw_s


def _dft_tables(n):
    idx = jnp.arange(n, dtype=jnp.int32)
    jk = (idx[:, None] * idx[None, :]) % n
    ang = jk.astype(F32) * (2.0 * jnp.pi / n)
    scale = n ** -0.5
    return jnp.cos(ang) * scale, jnp.sin(ang) * scale


def _rope_tables(length):
    dim = QK_ROPE_DIM
    half = dim // 2
    inv = 1.0 / (ROPE_THETA ** (jnp.arange(0, dim, 2, dtype=F32) / dim))
    ang = jnp.arange(length, dtype=F32)[:, None] * inv[None, :]
    cos, sin = jnp.cos(ang), jnp.sin(ang)
    zero = jnp.zeros_like(cos)
    cos_t = jnp.concatenate([cos, cos, zero, zero], axis=-1)
    sin_lo = jnp.concatenate([-sin, zero, zero, zero], axis=-1)
    sin_hi = jnp.concatenate([zero, sin, zero, zero], axis=-1)
    return cos_t, sin_lo, sin_hi, cos.T, sin.T


def _pad_cols(a, n):
    return jnp.concatenate([a, jnp.zeros(a.shape[:-1] + (n,), a.dtype)], axis=-1)


def _row(v):
    return v.reshape(1, -1).astype(F32)


class _Tiles(NamedTuple):
    token: int
    mla_pre: int
    attn: int
    scan: int
    hgrn_cols: int
    dft_inner: int
    dft_slabs: int
    ff: int


def _tiles(l):
    dft_inner = 64 if l >= 1024 else 16
    return _Tiles(token=min(l, 512), mla_pre=min(l, 256), attn=min(l, 1024), scan=min(l, 1024),
                  hgrn_cols=2048, dft_inner=dft_inner, dft_slabs=min(8, l // dft_inner), ff=1024)


def kernel(x_prompt, x_sample, mem_prompt, mem_sample, norm_mix_g, mla_w_in, mla_q_a_norm_g, mla_kv_a_norm_g, mla_w_q_b, mla_w_kv_b, mla_q_norm_g, mla_k_norm_g, mla_w_o, hgrn_w_in, hgrn_lb_logits, hgrn_o_norm_g, hgrn_w_o, fnet_w_o, norm_xq_g, norm_mem_g, mem_w_q, mem_w_kv, mem_q_norm_g, mem_k_norm_g, mem_w_o, norm_mlp_g, mlp_w1, mlp_w2):
    depth = norm_mix_g.shape[0]
    n_mixers = 3

    lb_all = jnp.cumsum(jax.nn.softmax(hgrn_lb_logits.astype(F32), axis=0), axis=0)
    lb_all = lb_all - lb_all[:1]

    mla_w_in_p = _pad_cols(mla_w_in, LANE - QK_ROPE_DIM).astype(BF16)
    n_mla = mla_w_in.shape[0]
    wqb = mla_w_q_b.reshape(n_mla, Q_LORA_RANK, MLA_HEADS, QK_HEAD_DIM)
    mla_w_qb_p = _pad_cols(wqb, QK_PAD_DIM - QK_HEAD_DIM).reshape(n_mla, Q_LORA_RANK, MLA_HEADS * QK_PAD_DIM).astype(BF16)
    mla_w_qt = jnp.swapaxes(mla_w_qb_p, 1, 2)
    kvb = mla_w_kv_b.reshape(n_mla, KV_LORA_RANK, MLA_HEADS, QK_NOPE_DIM + V_HEAD_DIM)
    mla_w_k = kvb[..., :QK_NOPE_DIM].reshape(n_mla, KV_LORA_RANK, MLA_HEADS * QK_NOPE_DIM).astype(BF16)
    mla_w_vt = jnp.swapaxes(kvb[..., QK_NOPE_DIM:].reshape(n_mla, KV_LORA_RANK, MLA_HEADS * V_HEAD_DIM),
                            1, 2).astype(BF16)
    mla_gq_t = jnp.broadcast_to((mla_q_norm_g * (QK_HEAD_DIM ** -0.5 * LOG2_E))[:, :, None],
                                (n_mla, QK_HEAD_DIM, LANE))
    mla_score_bound = (QK_HEAD_DIM ** 0.5 * LOG2_E * 1.02) * (
        jnp.max(jnp.abs(mla_q_norm_g), axis=-1) * jnp.max(jnp.abs(mla_k_norm_g), axis=-1))
    mla_gk_p = _pad_cols(mla_k_norm_g, QK_PAD_DIM - QK_HEAD_DIM)
    mla_w_o_b = mla_w_o.astype(BF16)
    hgrn_w_in_b = hgrn_w_in.astype(BF16)
    hgrn_w_o_b = hgrn_w_o.astype(BF16)
    fnet_w_o_b = fnet_w_o.astype(BF16)
    mem_w_q_b = mem_w_q.astype(BF16)
    mem_w_kv_b = mem_w_kv.astype(BF16)
    mem_w_o_b = mem_w_o.astype(BF16)
    mem_qg_scaled = mem_q_norm_g * (MEM_HEAD_DIM ** -0.5)
    mlp_w1_b = mlp_w1.astype(BF16)
    mlp_w2_b = mlp_w2.astype(BF16)

    has_fnet = depth >= n_mixers
    if has_fnet:
        cn, sn = _dft_tables(FNET_GROUP_DIM)
        cn, sn = cn.astype(BF16), sn.astype(BF16)

    def run_trunk(x, mem):
        b, l, d = x.shape
        t = b * l
        tiles = _tiles(l)
        tm = tiles.token
        if depth >= 1:
            rope_tabs = _rope_tables(l)
        if has_fnet:
            l2 = tiles.dft_inner
            l1 = l // l2
            m1, m2, tw_c, tw_s = _fnet_seq_tables(l1, l2)
        for i in range(depth):
            kind = i % n_mixers
            j = i // n_mixers
            gmix = _row(norm_mix_g[i])
            if kind == 0:
                q, k, v = mla_pre(x, gmix, mla_w_in_p[j], _row(mla_q_a_norm_g[j]), _row(mla_kv_a_norm_g[j]),
                                  mla_w_qt[j], mla_w_k[j], mla_w_vt[j], mla_gq_t[j], _row(mla_gk_p[j]),
                                  rope_tabs, tm=tiles.mla_pre)
                o = lax.cond(mla_score_bound[j] <= SCORE_LOG2_LIMIT,
                             functools.partial(flash_attention_bounded, tq=tiles.attn, tk=tiles.attn),
                             functools.partial(flash_attention, tq=tiles.attn, tk=tiles.attn), q, k, v)
                x = matmul_residual(o.reshape(t, d), mla_w_o_b[j], x.reshape(t, d), tm).reshape(b, l, d)
            elif kind == 1:
                proj = norm_matmul(x.reshape(t, d), gmix, hgrn_w_in_b[j], F32, tm, tiles.hgrn_cols).reshape(b, l, -1)
                tl = tiles.scan
                o_f = hgrn_scan(proj, _row(lb_all[i, 0]), False, tl)
                y = hgrn_scan(proj, _row(lb_all[i, 1]), True, tl, o_fwd=o_f, o_norm_g=_row(hgrn_o_norm_g[j]))
                x = matmul_residual(y.reshape(t, d), hgrn_w_o_b[j], x.reshape(t, d), tm).reshape(b, l, d)
            else:
                u, w = fnet_channel(x.reshape(t, d), gmix, cn, sn, tm)
                a_re, a_im = fnet_stage1(m1, u.reshape(b, l, d), w.reshape(b, l, d), l1, l2, tiles.dft_slabs)
                x = fnet_stage2(a_re, a_im, tw_c, tw_s, m2, x, fnet_w_o_b[j], l1, l2, tiles.dft_slabs)
            mk, mv = mem_kv(mem, _row(norm_mem_g[i]), mem_w_kv_b[i], _row(mem_k_norm_g[i]))
            x = mem_xattn(x, _row(norm_xq_g[i]), mem_w_q_b[i], _row(mem_qg_scaled[i]), mk, mv, mem_w_o_b[i], tm)
            x = mlp(x.reshape(t, d), _row(norm_mlp_g[i]), mlp_w1_b[i], mlp_w2_b[i], tm, tiles.ff).reshape(b, l, d)
        return x

    return (run_trunk(x_prompt, mem_prompt), run_trunk(x_sample, mem_sample))
```

```python
import functools
from typing import NamedTuple

import jax
import jax.numpy as jnp
from jax import lax
from jax.experimental import pallas as pl
from jax.experimental.pallas import tpu as pltpu

F32 = jnp.float32
BF16 = jnp.bfloat16

EPS = 1e-6
D_MODEL = 2048
D_FF = 4 * D_MODEL

MLA_HEADS = 16
Q_LORA_RANK = 512
KV_LORA_RANK = 512
QK_NOPE_DIM = 128
QK_ROPE_DIM = 64
QK_HEAD_DIM = QK_NOPE_DIM + QK_ROPE_DIM
V_HEAD_DIM = 128
ROPE_THETA = 10000.0
QK_PAD_DIM = 256
SCORE_LOG2_LIMIT = 100.0
ONES_ROWS = 16

HGRN_HEADS = 16
HGRN_KEY_DIM = 128
HGRN_VAL_DIM = D_MODEL // HGRN_HEADS
HGRN_KEY_WIDTH = HGRN_HEADS * HGRN_KEY_DIM
HGRN_CHUNK = 64
HGRN_SUB = 8
LOG2_E = 1.4426950408889634

FNET_GROUPS = 4
FNET_GROUP_DIM = D_MODEL // FNET_GROUPS

MEM_HEADS = 4
MEM_HEAD_DIM = 128
MEM_WIDTH = MEM_HEADS * MEM_HEAD_DIM

LANE = 128
VMEM_LIMIT_BYTES = 56 * 1024 * 1024


def _params(*semantics):
    return pltpu.CompilerParams(dimension_semantics=semantics, vmem_limit_bytes=VMEM_LIMIT_BYTES)


def _resident(shape):
    nd = len(shape)
    return pl.BlockSpec(shape, lambda *_: (0,) * nd, pipeline_mode=pl.Buffered(1))


def _rms(x, g):
    ms = jnp.mean(x * x, axis=-1, keepdims=True)
    return x * lax.rsqrt(ms + EPS) * g


def _dot(a, b):
    return jnp.dot(a, b, preferred_element_type=F32)


def _dot_nt(a, b):
    return lax.dot_general(a, b, (((1,), (1,)), ((), ())), preferred_element_type=F32)


def _norm_matmul_kernel(x_ref, g_ref, w_ref, o_ref, xn_ref):
    @pl.when(pl.program_id(1) == 0)
    def _():
        xn_ref[...] = _rms(x_ref[...], g_ref[...]).astype(BF16)

    o_ref[...] = _dot(xn_ref[...], w_ref[...]).astype(o_ref.dtype)


def norm_matmul(x, g, w, out_dtype, tm, tn):
    t, d = x.shape
    n = w.shape[1]
    return pl.pallas_call(
        _norm_matmul_kernel,
        grid=(t // tm, n // tn),
        in_specs=[
            pl.BlockSpec((tm, d), lambda i, j: (i, 0)),
            _resident((1, d)),
            pl.BlockSpec((d, tn), lambda i, j: (0, j)),
        ],
        out_specs=pl.BlockSpec((tm, tn), lambda i, j: (i, j)),
        out_shape=jax.ShapeDtypeStruct((t, n), out_dtype),
        scratch_shapes=[pltpu.VMEM((tm, d), BF16)],
        compiler_params=_params("parallel", "arbitrary"),
        name="norm_matmul",
    )(x, g, w)


def _matmul_residual_kernel(a_ref, w_ref, r_ref, o_ref):
    o_ref[...] = r_ref[...] + _dot(a_ref[...], w_ref[...])


def matmul_residual(a, w, res, tm):
    t, k = a.shape
    n = w.shape[1]
    return pl.pallas_call(
        _matmul_residual_kernel,
        grid=(t // tm,),
        in_specs=[
            pl.BlockSpec((tm, k), lambda i: (i, 0)),
            _resident((k, n)),
            pl.BlockSpec((tm, n), lambda i: (i, 0)),
        ],
        out_specs=pl.BlockSpec((tm, n), lambda i: (i, 0)),
        out_shape=jax.ShapeDtypeStruct((t, n), F32),
        compiler_params=_params("parallel"),
        name="matmul_residual",
    )(a, w, res)


def _mlp_kernel(x_ref, g_ref, w1_ref, w2_ref, o_ref, xn_ref, acc_ref):
    j = pl.program_id(1)

    @pl.when(j == 0)
    def _():
        xn_ref[...] = _rms(x_ref[...], g_ref[...]).astype(BF16)
        acc_ref[...] = jnp.zeros_like(acc_ref)

    h = jnp.maximum(_dot(xn_ref[...], w1_ref[...]), 0.0)
    acc_ref[...] += _dot((h * h).astype(BF16), w2_ref[...])

    @pl.when(j == pl.num_programs(1) - 1)
    def _():
        o_ref[...] = x_ref[...] + acc_ref[...]


def mlp(x, g, w1, w2, tm, tf):
    t, d = x.shape
    f = w1.shape[1]
    return pl.pallas_call(
        _mlp_kernel,
        grid=(t // tm, f // tf),
        in_specs=[
            pl.BlockSpec((tm, d), lambda i, j: (i, 0)),
            _resident((1, d)),
            pl.BlockSpec((d, tf), lambda i, j: (0, j)),
            pl.BlockSpec((tf, d), lambda i, j: (j, 0)),
        ],
        out_specs=pl.BlockSpec((tm, d), lambda i, j: (i, 0)),
        out_shape=jax.ShapeDtypeStruct((t, d), F32),
        scratch_shapes=[pltpu.VMEM((tm, d), BF16), pltpu.VMEM((tm, d), F32)],
        compiler_params=_params("parallel", "arbitrary"),
        name="mlp",
    )(x, g, w1, w2)


def _mem_kv_kernel(m_ref, g_ref, w_ref, kg_ref, k_out, v_out):
    mn = _rms(m_ref[...], g_ref[...]).astype(BF16)
    kv = _dot(mn, w_ref[...])
    kg = kg_ref[...]
    for h in range(MEM_HEADS):
        lo = h * MEM_HEAD_DIM
        k_out[:, lo:lo + MEM_HEAD_DIM] = _rms(kv[:, lo:lo + MEM_HEAD_DIM], kg).astype(BF16)
    v_out[...] = kv[:, MEM_WIDTH:].astype(BF16)


def mem_kv(mem, g, w_kv, k_g):
    b, m, d = mem.shape
    spec_out = pl.BlockSpec((None, m, MEM_WIDTH), lambda i: (i, 0, 0))
    return pl.pallas_call(
        _mem_kv_kernel,
        grid=(b,),
        in_specs=[
            pl.BlockSpec((None, m, d), lambda i: (i, 0, 0)),
            _resident((1, d)),
            _resident((d, 2 * MEM_WIDTH)),
            _resident((1, MEM_HEAD_DIM)),
        ],
        out_specs=[spec_out, spec_out],
        out_shape=[jax.ShapeDtypeStruct((b, m, MEM_WIDTH), BF16)] * 2,
        compiler_params=_params("parallel"),
        name="mem_kv",
    )(mem, g, w_kv, k_g)


def _mem_xattn_kernel(x_ref, g_ref, wq_ref, qg_ref, k_ref, v_ref, wo_ref, o_ref):
    x = x_ref[...]
    q = _dot(_rms(x, g_ref[...]).astype(BF16), wq_ref[...])
    qg = qg_ref[...]
    heads = []
    for h in range(MEM_HEADS):
        lo = h * MEM_HEAD_DIM
        qh = _rms(q[:, lo:lo + MEM_HEAD_DIM], qg).astype(BF16)
        s = _dot_nt(qh, k_ref[:, lo:lo + MEM_HEAD_DIM])
        e = jnp.exp(s - jnp.max(s, axis=-1, keepdims=True))
        p = e / jnp.sum(e, axis=-1, keepdims=True)
        heads.append(_dot(p.astype(BF16), v_ref[:, lo:lo + MEM_HEAD_DIM]))
    o = jnp.concatenate(heads, axis=-1).astype(BF16)
    o_ref[...] = x + _dot(o, wo_ref[...])


def mem_xattn(x, g, w_q, q_g_scaled, k, v, w_o, tm):
    b, l, d = x.shape
    m = k.shape[1]
    spec_x = pl.BlockSpec((None, tm, d), lambda i, j: (i, j, 0))
    spec_kv = pl.BlockSpec((None, m, MEM_WIDTH), lambda i, j: (i, 0, 0))
    return pl.pallas_call(
        _mem_xattn_kernel,
        grid=(b, l // tm),
        in_specs=[
            spec_x,
            _resident((1, d)),
            _resident((d, MEM_WIDTH)),
            _resident((1, MEM_HEAD_DIM)),
            spec_kv,
            spec_kv,
            _resident((MEM_WIDTH, d)),
        ],
        out_specs=spec_x,
        out_shape=jax.ShapeDtypeStruct((b, l, d), F32),
        compiler_params=_params("parallel", "parallel"),
        name="mem_xattn",
    )(x, g, w_q, q_g_scaled, k, v, w_o)


def _mla_pre_kernel(x_ref, gmix_ref, win_ref, qag_ref, kvag_ref, wqt_ref, wk_ref, wvt_ref, gqt_ref, gk_ref,
                    cosk_ref, sin_lo_ref, sin_hi_ref, cosq_ref, sinq_ref, q_out, k_out, v_out):
    tm = x_ref.shape[0]
    half = QK_ROPE_DIM // 2
    xn = _rms(x_ref[...], gmix_ref[...]).astype(BF16)
    proj = _dot(xn, win_ref[...])
    ql = _rms(proj[:, :Q_LORA_RANK], qag_ref[...])
    kvl = _rms(proj[:, Q_LORA_RANK:Q_LORA_RANK + KV_LORA_RANK], kvag_ref[...])
    kr = proj[:, Q_LORA_RANK + KV_LORA_RANK:]
    kvl_t = kvl.T.astype(BF16)
    qt = _dot(wqt_ref[...], ql.T.astype(BF16))
    vt = _dot(wvt_ref[...], kvl_t)
    kn_all = _dot(kvl.astype(BF16), wk_ref[...])
    inv_dim = 1.0 / QK_HEAD_DIM

    gk = gk_ref[...]
    ss_kr = jnp.sum(kr * kr, axis=-1, keepdims=True)
    krg = kr * gk[:, QK_NOPE_DIM:]
    kr_rot = (krg * cosk_ref[...] + pltpu.roll(krg, half, 1) * sin_hi_ref[...]
              + pltpu.roll(krg, LANE - half, 1) * sin_lo_ref[...])
    reps = tm // LANE
    gqt = jnp.tile(gqt_ref[...], (1, reps))
    cos_q = cosq_ref[...]
    sin_q = sinq_ref[...]
    zero_rows = jnp.zeros((QK_PAD_DIM - QK_HEAD_DIM, tm), BF16)
    for h in range(MLA_HEADS):
        qh = qt[h * QK_PAD_DIM:h * QK_PAD_DIM + QK_HEAD_DIM, :]
        inv_q = lax.rsqrt(jnp.sum(qh * qh, axis=0, keepdims=True) * inv_dim + EPS)
        qn = qh * inv_q * gqt
        x1 = qn[QK_NOPE_DIM:QK_NOPE_DIM + half, :]
        x2 = qn[QK_NOPE_DIM + half:, :]
        q_out[h, :QK_NOPE_DIM, :] = qn[:QK_NOPE_DIM, :].astype(BF16)
        q_out[h, QK_NOPE_DIM:QK_NOPE_DIM + half, :] = (x1 * cos_q - x2 * sin_q).astype(BF16)
        q_out[h, QK_NOPE_DIM + half:QK_HEAD_DIM, :] = (x1 * sin_q + x2 * cos_q).astype(BF16)
        q_out[h, QK_HEAD_DIM:, :] = zero_rows
        kn = kn_all[:, h * QK_NOPE_DIM:(h + 1) * QK_NOPE_DIM]
        inv_k = lax.rsqrt((jnp.sum(kn * kn, axis=-1, keepdims=True) + ss_kr) * inv_dim + EPS)
        k_out[h, :, :QK_NOPE_DIM] = (kn * inv_k * gk[:, :QK_NOPE_DIM]).astype(BF16)
        k_out[h, :, QK_NOPE_DIM:] = (kr_rot * inv_k).astype(BF16)
        v_out[h] = vt[h * V_HEAD_DIM:(h + 1) * V_HEAD_DIM, :].astype(BF16)


def mla_pre(x, gmix, w_in_p, qag, kvag, w_qt, w_k, w_vt, gq_t, gk_p, rope_tabs, tm):
    b, l, d = x.shape
    h = MLA_HEADS
    cos_k, sin_lo, sin_hi, cos_q, sin_q = rope_tabs
    spec_tab_k = pl.BlockSpec((tm, LANE), lambda i, j: (j, 0))
    spec_tab_q = pl.BlockSpec((QK_ROPE_DIM // 2, tm), lambda i, j: (0, j))
    spec_qv = lambda rows: pl.BlockSpec((None, h, rows, tm), lambda i, j: (i, 0, 0, j))
    return pl.pallas_call(
        _mla_pre_kernel,
        grid=(b, l // tm),
        in_specs=[
            pl.BlockSpec((None, tm, d), lambda i, j: (i, j, 0)),
            _resident((1, d)),
            _resident(w_in_p.shape),
            _resident((1, Q_LORA_RANK)),
            _resident((1, KV_LORA_RANK)),
            _resident(w_qt.shape),
            _resident(w_k.shape),
            _resident(w_vt.shape),
            _resident(gq_t.shape),
            _resident((1, QK_PAD_DIM)),
            spec_tab_k, spec_tab_k, spec_tab_k, spec_tab_q, spec_tab_q,
        ],
        out_specs=[spec_qv(QK_PAD_DIM), pl.BlockSpec((None, h, tm, QK_PAD_DIM), lambda i, j: (i, 0, j, 0)),
                   spec_qv(V_HEAD_DIM)],
        out_shape=[
            jax.ShapeDtypeStruct((b, h, QK_PAD_DIM, l), BF16),
            jax.ShapeDtypeStruct((b, h, l, QK_PAD_DIM), BF16),
            jax.ShapeDtypeStruct((b, h, V_HEAD_DIM, l), BF16),
        ],
        compiler_params=_params("parallel", "parallel"),
        name="mla_pre",
    )(x, gmix, w_in_p, qag, kvag, w_qt, w_k, w_vt, gq_t, gk_p, cos_k, sin_lo, sin_hi, cos_q, sin_q)


def _flash_kernel(q_ref, k_ref, vt_ref, o_ref, m_ref, l_ref, acc_ref):
    kv = pl.program_id(3)

    @pl.when(kv == 0)
    def _():
        m_ref[...] = jnp.full_like(m_ref, -jnp.inf)
        l_ref[...] = jnp.zeros_like(l_ref)
        acc_ref[...] = jnp.zeros_like(acc_ref)

    s = _dot(k_ref[...], q_ref[...])
    m_prev = m_ref[...]
    m_new = jnp.maximum(m_prev, jnp.max(s, axis=0, keepdims=True))
    alpha = jnp.exp2(m_prev - m_new)
    p = jnp.exp2(s - m_new)
    l_ref[...] = alpha * l_ref[...] + jnp.sum(p, axis=0, keepdims=True)
    acc_ref[...] = alpha * acc_ref[...] + _dot(vt_ref[...], p.astype(BF16))
    m_ref[...] = m_new

    @pl.when(kv == pl.num_programs(3) - 1)
    def _():
        o_ref[...] = (acc_ref[...] / l_ref[...]).T.astype(o_ref.dtype)


def flash_attention(qt, k, vt, tq, tk):
    b, h, dk, l = qt.shape
    dv = vt.shape[2]
    return pl.pallas_call(
        _flash_kernel,
        grid=(b, h, l // tq, l // tk),
        in_specs=[
            pl.BlockSpec((None, None, dk, tq), lambda i, j, a, c: (i, j, 0, a)),
            pl.BlockSpec((None, None, tk, dk), lambda i, j, a, c: (i, j, c, 0)),
            pl.BlockSpec((None, None, dv, tk), lambda i, j, a, c: (i, j, 0, c)),
        ],
        out_specs=pl.BlockSpec((None, tq, dv), lambda i, j, a, c: (i, a, j)),
        out_shape=jax.ShapeDtypeStruct((b, l, h * dv), BF16),
        scratch_shapes=[pltpu.VMEM((1, tq), F32), pltpu.VMEM((1, tq), F32), pltpu.VMEM((dv, tq), F32)],
        compiler_params=_params("parallel", "parallel", "parallel", "arbitrary"),
        name="flash_attention",
    )(qt, k, vt)


def _flash_bounded_kernel(q_ref, k_ref, vt_ref, o_ref, acc_ref, *, tk):
    dv = vt_ref.shape[0]
    q = q_ref[...]
    ones = jnp.ones((ONES_ROWS, tk), BF16)
    for c in range(k_ref.shape[0] // tk):
        cols = slice(c * tk, (c + 1) * tk)
        p = jnp.exp2(_dot(k_ref[cols, :], q)).astype(BF16)
        pv = _dot(jnp.concatenate([vt_ref[:, cols], ones], axis=0), p)
        if c == 0:
            acc_ref[...] = pv
        else:
            acc_ref[...] += pv
    acc = acc_ref[...]
    o_ref[...] = (acc[:dv] / acc[dv:dv + 1]).T.astype(o_ref.dtype)


def flash_attention_bounded(qt, k, vt, tq, tk):
    b, h, dk, l = qt.shape
    dv = vt.shape[2]
    return pl.pallas_call(
        functools.partial(_flash_bounded_kernel, tk=tk),
        grid=(b, h, l // tq),
        in_specs=[
            pl.BlockSpec((None, None, dk, tq), lambda i, j, a: (i, j, 0, a)),
            pl.BlockSpec((None, None, l, dk), lambda i, j, a: (i, j, 0, 0)),
            pl.BlockSpec((None, None, dv, l), lambda i, j, a: (i, j, 0, 0)),
        ],
        out_specs=pl.BlockSpec((None, tq, dv), lambda i, j, a: (i, a, j)),
        out_shape=jax.ShapeDtypeStruct((b, l, h * dv), BF16),
        scratch_shapes=[pltpu.VMEM((dv + ONES_ROWS, tq), F32)],
        compiler_params=_params("parallel", "parallel", "arbitrary"),
        name="flash_attention_bounded",
    )(qt, k, vt)


def _hgrn_scan_kernel(*refs, reverse, n_chunks):
    if reverse:
        q_ref, z_ref, v_ref, lb_ref, sel_ref, of_ref, gate_ref, ong_ref, o_ref, s_ref, b_ref, kk_ref, lhs_ref = refs
    else:
        q_ref, z_ref, v_ref, lb_ref, sel_ref, o_ref, s_ref, b_ref, kk_ref, lhs_ref = refs
    c = HGRN_CHUNK
    sub = HGRN_SUB
    per = c // sub
    kd = HGRN_KEY_DIM
    tl = n_chunks * c

    @pl.when(pl.program_id(2) == 0)
    def _():
        s_ref[...] = jnp.zeros_like(s_ref)

    lb = lb_ref[...]
    z = z_ref[...]
    e = jnp.exp(-jnp.abs(z))
    r = 1.0 / (1.0 + e)
    f = lb + (1.0 - lb) * (jnp.where(z >= 0.0, 1.0, e) * r)
    log2_f = jnp.where(f > 0.0, jnp.log2(f), (z + jnp.log1p(-lb)) * LOG2_E)
    kk = (1.0 - lb) * (jnp.where(z >= 0.0, e, 1.0) * r)

    row = lax.broadcasted_iota(jnp.int32, (c, c), 0)
    col = lax.broadcasted_iota(jnp.int32, (c, c), 1)
    visible = (col >= row) if reverse else (col <= row)
    tri_f = visible.astype(F32)
    for ci in range(n_chunks):
        bsum = jnp.dot(tri_f, log2_f[ci * c:(ci + 1) * c], precision=lax.Precision.HIGHEST,
                       preferred_element_type=F32)
        b_ref[ci * per:(ci + 1) * per] = bsum.reshape(per, sub, kd)
    kk_ref[...] = kk.reshape(tl // sub, sub, kd)

    q3 = q_ref[...].reshape(tl // sub, sub, kd)
    b3 = b_ref[...]
    for s in range(sub):
        k_row = kk_ref[:, s:s + 1, :]
        b_row = b_ref[:, s:s + 1, :]
        w = q3 * k_row * jnp.exp2(jnp.minimum(b3 - b_row, 0.0))
        lhs_ref[:, s * kd:(s + 1) * kd] = w.reshape(tl, kd).astype(BF16)
    d_all = _dot(lhs_ref[...], sel_ref[...])

    q2 = q_ref[...]
    k2 = kk.reshape(tl, kd)
    b2 = b3.reshape(tl, kd)

    split_q, split_k, split_mask = [], [], []
    group = c
    while group > sub:
        half = group // 2
        pos = lax.broadcasted_iota(jnp.int32, (1, group, 1), 1)
        is_q = (pos < half) if reverse else (pos >= half)
        bg = b2.reshape(tl // group, group, kd)
        split_row = half if reverse else half - 1
        b_split = bg[:, split_row:split_row + 1, :]
        dec = jnp.exp2(jnp.where(is_q, bg - b_split, b_split - bg))
        split_q.append(jnp.where(is_q, q2.reshape(bg.shape) * dec, 0.0).astype(BF16).reshape(tl, kd))
        split_k.append(jnp.where(is_q, 0.0, k2.reshape(bg.shape) * dec).astype(BF16).reshape(tl, kd))
        split_mask.append(None if group == c else jnp.where((row // group) == (col // group), 1.0, 0.0))
        group = half
    mask_d = jnp.where(((row // sub) == (col // sub)) & visible, 1.0, 0.0)

    edge = 0 if reverse else c - 1
    bc3 = b2.reshape(n_chunks, c, kd)
    b_edge = bc3[:, edge:edge + 1, :]
    q_in = (q2.reshape(bc3.shape) * jnp.exp2(bc3)).astype(BF16)
    k_out = (k2.reshape(bc3.shape) * jnp.exp2(b_edge - bc3)).astype(BF16)
    edge_decay = jnp.exp2(b_edge)

    order = range(n_chunks - 1, -1, -1) if reverse else range(n_chunks)
    o_intra, s_add = {}, {}
    for ci in order:
        rows = slice(ci * c, (ci + 1) * c)
        att = d_all[rows, :] * mask_d
        for qs, ks, mask in zip(split_q, split_k, split_mask):
            part = _dot_nt(qs[rows, :], ks[rows, :])
            att = att + (part if mask is None else part * mask)
        vc = v_ref[rows, :]
        o_intra[ci] = _dot(att.astype(BF16), vc.astype(BF16))
        s_add[ci] = _dot(vc.T.astype(BF16), k_out[ci])

    st = s_ref[...]
    for ci in order:
        rows = slice(ci * c, (ci + 1) * c)
        o = o_intra[ci] + _dot_nt(q_in[ci], st.astype(BF16))
        st = st * edge_decay[ci] + s_add[ci]
        if reverse:
            o = _rms(o + of_ref[rows, :], ong_ref[...])
            g = gate_ref[rows, :]
            o_ref[rows, :] = (o * (g * jax.nn.sigmoid(g))).astype(o_ref.dtype)
        else:
            o_ref[rows, :] = o
    s_ref[...] = st


def hgrn_scan(proj, lb, reverse, tl, o_fwd=None, o_norm_g=None):
    b, l, _ = proj.shape
    h = HGRN_HEADS
    nl = l // tl
    n_chunks = tl // HGRN_CHUNK
    kd = HGRN_KEY_DIM

    def pos(j):
        return (nl - 1 - j) if reverse else j

    def col_spec(section):
        return pl.BlockSpec((None, tl, kd), lambda i, hh, j: (i, pos(j), section * h + hh))

    sel_s = lax.broadcasted_iota(jnp.int32, (HGRN_SUB, kd, HGRN_CHUNK), 0)
    sel_j = lax.broadcasted_iota(jnp.int32, (HGRN_SUB, kd, HGRN_CHUNK), 2)
    sel = (sel_j % HGRN_SUB == sel_s).astype(BF16).reshape(HGRN_SUB * kd, HGRN_CHUNK)

    in_specs = [col_spec(0), col_spec(2 if reverse else 1), col_spec(3),
                pl.BlockSpec((1, kd), lambda i, hh, j: (0, hh)), _resident(sel.shape)]
    args = [proj, proj, proj, lb, sel]
    out_spec = pl.BlockSpec((None, tl, kd), lambda i, hh, j: (i, pos(j), hh))
    if reverse:
        in_specs += [out_spec, col_spec(4), _resident((1, HGRN_VAL_DIM))]
        args += [o_fwd, proj, o_norm_g]
    return pl.pallas_call(
        functools.partial(_hgrn_scan_kernel, reverse=reverse, n_chunks=n_chunks),
        grid=(b, h, nl),
        in_specs=in_specs,
        out_specs=out_spec,
        out_shape=jax.ShapeDtypeStruct((b, l, D_MODEL), BF16 if reverse else F32),
        scratch_shapes=[
            pltpu.VMEM((HGRN_VAL_DIM, kd), F32),
            pltpu.VMEM((tl // HGRN_SUB, HGRN_SUB, kd), F32),
            pltpu.VMEM((tl // HGRN_SUB, HGRN_SUB, kd), F32),
            pltpu.VMEM((tl, HGRN_SUB * kd), BF16),
        ],
        compiler_params=_params("parallel", "parallel", "arbitrary"),
        name="hgrn_scan_bwd" if reverse else "hgrn_scan_fwd",
    )(*args)


def _fnet_channel_kernel(x_ref, g_ref, cn_ref, sn_ref, u_ref, w_ref):
    xn = _rms(x_ref[...], g_ref[...]).astype(BF16)
    for gi in range(FNET_GROUPS):
        cols = slice(gi * FNET_GROUP_DIM, (gi + 1) * FNET_GROUP_DIM)
        u_ref[:, cols] = _dot(xn[:, cols], cn_ref[...]).astype(BF16)
        w_ref[:, cols] = _dot(xn[:, cols], sn_ref[...]).astype(BF16)


def fnet_channel(x, g, cn, sn, tm):
    t, d = x.shape
    spec = pl.BlockSpec((tm, d), lambda i: (i, 0))
    return pl.pallas_call(
        _fnet_channel_kernel,
        grid=(t // tm,),
        in_specs=[spec, _resident((1, d)), _resident(cn.shape), _resident(sn.shape)],
        out_specs=[spec, spec],
        out_shape=[jax.ShapeDtypeStruct((t, d), BF16)] * 2,
        compiler_params=_params("parallel"),
        name="fnet_channel",
    )(x, g, cn, sn)


def _fnet_stage1_kernel(m1_ref, u_ref, w_ref, are_ref, aim_ref, uf_ref, wf_ref, ar_ref, ai_ref):
    l1, n_slabs, _ = u_ref.shape
    m1 = m1_ref[...]
    uf_ref[...] = u_ref[...].astype(F32)
    wf_ref[...] = w_ref[...].astype(F32)
    for s in range(n_slabs):
        z = jnp.concatenate([uf_ref[:, s, :], wf_ref[:, s, :]], axis=0).astype(BF16)
        a = _dot(m1, z)
        ar_ref[:, s, :] = a[:l1]
        ai_ref[:, s, :] = a[l1:]
    are_ref[...] = ar_ref[...].astype(BF16)
    aim_ref[...] = ai_ref[...].astype(BF16)


def fnet_stage1(m1, u, w, l1, l2, n_slabs, cb):
    b, l, d = u.shape
    spec = pl.BlockSpec((None, l1, n_slabs, cb), lambda i, j, c: (i, 0, j, c))
    a_re, a_im = pl.pallas_call(
        _fnet_stage1_kernel,
        grid=(b, l2 // n_slabs, d // cb),
        in_specs=[_resident(m1.shape), spec, spec],
        out_specs=[spec, spec],
        out_shape=[jax.ShapeDtypeStruct((b, l1, l2, d), BF16)] * 2,
        scratch_shapes=[pltpu.VMEM((l1, n_slabs, cb), F32)] * 4,
        compiler_params=_params("parallel", "parallel", "parallel"),
        name="fnet_stage1",
    )(m1, u.reshape(b, l1, l2, d), w.reshape(b, l1, l2, d))
    return a_re, a_im


def _fnet_stage2_kernel(are_ref, aim_ref, twc_ref, tws_ref, m2_ref, x_ref, wo_ref, o_ref, y_ref):
    n_slabs, l2, d = are_ref.shape
    m2 = m2_ref[...]
    for j in range(n_slabs):
        a_re = are_ref[j].astype(F32)
        a_im = aim_ref[j].astype(F32)
        c = jnp.tile(twc_ref[j], (1, d // LANE))
        s = jnp.tile(tws_ref[j], (1, d // LANE))
        b_re = a_re * c + a_im * s
        b_im = a_im * c - a_re * s
        y = _dot(m2, jnp.concatenate([b_re, b_im], axis=0).astype(BF16))
        y_ref[j * l2:(j + 1) * l2, :] = y.astype(BF16)
    out = _dot(y_ref[...], wo_ref[...])
    for j in range(n_slabs):
        o_ref[:, j, :] = x_ref[:, j, :] + out[j * l2:(j + 1) * l2, :]


def fnet_stage2(a_re, a_im, tw_c, tw_s, m2, x, w_o, l1, l2, n_slabs):
    b, l, d = x.shape
    spec_a = pl.BlockSpec((None, n_slabs, l2, d), lambda i, j: (i, j, 0, 0))
    spec_tw = pl.BlockSpec((n_slabs, l2, LANE), lambda i, j: (j, 0, 0))
    spec_x = pl.BlockSpec((None, l2, n_slabs, d), lambda i, j: (i, 0, j, 0))
    out = pl.pallas_call(
        _fnet_stage2_kernel,
        grid=(b, l1 // n_slabs),
        in_specs=[spec_a, spec_a, spec_tw, spec_tw, _resident(m2.shape), spec_x, _resident(w_o.shape)],
        out_specs=spec_x,
        out_shape=jax.ShapeDtypeStruct((b, l2, l1, d), F32),
        scratch_shapes=[pltpu.VMEM((n_slabs * l2, d), BF16)],
        compiler_params=_params("parallel", "parallel"),
        name="fnet_stage2",
    )(a_re, a_im, tw_c, tw_s, m2, x.reshape(b, l2, l1, d), w_o)
    return out.reshape(b, l, d)


def _fnet_seq_tables(l1, l2):
    c1, s1 = _dft_tables(l1)
    c2, s2 = _dft_tables(l2)
    m1 = jnp.concatenate([jnp.concatenate([c1, -s1], axis=1), jnp.concatenate([-s1, -c1], axis=1)], axis=0)
    m2 = jnp.concatenate([c2, s2], axis=1)
    k1 = jnp.arange(l1, dtype=jnp.int32)[:, None]
    n2 = jnp.arange(l2, dtype=jnp.int32)[None, :]
    ang = ((k1 * n2) % (l1 * l2)).astype(F32)[:, :, None] * (2.0 * jnp.pi / (l1 * l2))
    tw_c = jnp.broadcast_to(jnp.cos(ang), (l1, l2, LANE))
    tw_s = jnp.broadcast_to(jnp.sin(ang), (l1, l2, LANE))
    return m1.astype(BF16), m2.astype(BF16), tw_c, tw_s


def _dft_tables(n):
    idx = jnp.arange(n, dtype=jnp.int32)
    jk = (idx[:, None] * idx[None, :]) % n
    ang = jk.astype(F32) * (2.0 * jnp.pi / n)
    scale = n ** -0.5
    return jnp.cos(ang) * scale, jnp.sin(ang) * scale


def _rope_tables(length):
    dim = QK_ROPE_DIM
    half = dim // 2
    inv = 1.0 / (ROPE_THETA ** (jnp.arange(0, dim, 2, dtype=F32) / dim))
    ang = jnp.arange(length, dtype=F32)[:, None] * inv[None, :]
    cos, sin = jnp.cos(ang), jnp.sin(ang)
    zero = jnp.zeros_like(cos)
    cos_t = jnp.concatenate([cos, cos, zero, zero], axis=-1)
    sin_lo = jnp.concatenate([-sin, zero, zero, zero], axis=-1)
    sin_hi = jnp.concatenate([zero, sin, zero, zero], axis=-1)
    return cos_t, sin_lo, sin_hi, cos.T, sin.T


def _pad_cols(a, n):
    return jnp.concatenate([a, jnp.zeros(a.shape[:-1] + (n,), a.dtype)], axis=-1)


def _row(v):
    return v.reshape(1, -1).astype(F32)


class _Tiles(NamedTuple):
    token: int
    mla_pre: int
    attn: int
    scan: int
    hgrn_cols: int
    dft_inner: int
    dft_slabs1: int
    dft_cols1: int
    dft_slabs2: int
    ff: int


def _tiles(l):
    dft_inner = 64 if l >= 1024 else 16
    return _Tiles(token=min(l, 512), mla_pre=min(l, 256), attn=min(l, 1024), scan=min(l, 1024),
                  hgrn_cols=2048, dft_inner=dft_inner, dft_slabs1=16, dft_cols1=1024,
                  dft_slabs2=min(8, l // dft_inner), ff=1024)


def kernel(x_prompt, x_sample, mem_prompt, mem_sample, norm_mix_g, mla_w_in, mla_q_a_norm_g, mla_kv_a_norm_g, mla_w_q_b, mla_w_kv_b, mla_q_norm_g, mla_k_norm_g, mla_w_o, hgrn_w_in, hgrn_lb_logits, hgrn_o_norm_g, hgrn_w_o, fnet_w_o, norm_xq_g, norm_mem_g, mem_w_q, mem_w_kv, mem_q_norm_g, mem_k_norm_g, mem_w_o, norm_mlp_g, mlp_w1, mlp_w2):
    depth = norm_mix_g.shape[0]
    n_mixers = 3

    lb_all = jnp.cumsum(jax.nn.softmax(hgrn_lb_logits.astype(F32), axis=0), axis=0)
    lb_all = lb_all - lb_all[:1]

    mla_w_in_p = _pad_cols(mla_w_in, LANE - QK_ROPE_DIM).astype(BF16)
    n_mla = mla_w_in.shape[0]
    wqb = mla_w_q_b.reshape(n_mla, Q_LORA_RANK, MLA_HEADS, QK_HEAD_DIM)
    mla_w_qb_p = _pad_cols(wqb, QK_PAD_DIM - QK_HEAD_DIM).reshape(n_mla, Q_LORA_RANK, MLA_HEADS * QK_PAD_DIM).astype(BF16)
    mla_w_qt = jnp.swapaxes(mla_w_qb_p, 1, 2)
    kvb = mla_w_kv_b.reshape(n_mla, KV_LORA_RANK, MLA_HEADS, QK_NOPE_DIM + V_HEAD_DIM)
    mla_w_k = kvb[..., :QK_NOPE_DIM].reshape(n_mla, KV_LORA_RANK, MLA_HEADS * QK_NOPE_DIM).astype(BF16)
    mla_w_vt = jnp.swapaxes(kvb[..., QK_NOPE_DIM:].reshape(n_mla, KV_LORA_RANK, MLA_HEADS * V_HEAD_DIM),
                            1, 2).astype(BF16)
    mla_gq_t = jnp.broadcast_to((mla_q_norm_g * (QK_HEAD_DIM ** -0.5 * LOG2_E))[:, :, None],
                                (n_mla, QK_HEAD_DIM, LANE))
    mla_score_bound = (QK_HEAD_DIM ** 0.5 * LOG2_E * 1.02) * (
        jnp.max(jnp.abs(mla_q_norm_g), axis=-1) * jnp.max(jnp.abs(mla_k_norm_g), axis=-1))
    mla_gk_p = _pad_cols(mla_k_norm_g, QK_PAD_DIM - QK_HEAD_DIM)
    mla_w_o_b = mla_w_o.astype(BF16)
    hgrn_w_in_b = hgrn_w_in.astype(BF16)
    hgrn_w_o_b = hgrn_w_o.astype(BF16)
    fnet_w_o_b = fnet_w_o.astype(BF16)
    mem_w_q_b = mem_w_q.astype(BF16)
    mem_w_kv_b = mem_w_kv.astype(BF16)
    mem_w_o_b = mem_w_o.astype(BF16)
    mem_qg_scaled = mem_q_norm_g * (MEM_HEAD_DIM ** -0.5)
    mlp_w1_b = mlp_w1.astype(BF16)
    mlp_w2_b = mlp_w2.astype(BF16)

    has_fnet = depth >= n_mixers
    if has_fnet:
        cn, sn = _dft_tables(FNET_GROUP_DIM)
        cn, sn = cn.astype(BF16), sn.astype(BF16)

    def run_trunk(x, mem):
        b, l, d = x.shape
        t = b * l
        tiles = _tiles(l)
        tm = tiles.token
        if depth >= 1:
            rope_tabs = _rope_tables(l)
        if has_fnet:
            l2 = tiles.dft_inner
            l1 = l // l2
            m1, m2, tw_c, tw_s = _fnet_seq_tables(l1, l2)
        for i in range(depth):
            kind = i % n_mixers
            j = i // n_mixers
            gmix = _row(norm_mix_g[i])
            if kind == 0:
                q, k, v = mla_pre(x, gmix, mla_w_in_p[j], _row(mla_q_a_norm_g[j]), _row(mla_kv_a_norm_g[j]),
                                  mla_w_qt[j], mla_w_k[j], mla_w_vt[j], mla_gq_t[j], _row(mla_gk_p[j]),
                                  rope_tabs, tm=tiles.mla_pre)
                o = lax.cond(mla_score_bound[j] <= SCORE_LOG2_LIMIT,
                             functools.partial(flash_attention_bounded, tq=tiles.attn, tk=tiles.attn),
                             functools.partial(flash_attention, tq=tiles.attn, tk=tiles.attn), q, k, v)
                x = matmul_residual(o.reshape(t, d), mla_w_o_b[j], x.reshape(t, d), tm).reshape(b, l, d)
            elif kind == 1:
                proj = norm_matmul(x.reshape(t, d), gmix, hgrn_w_in_b[j], F32, tm, tiles.hgrn_cols).reshape(b, l, -1)
                tl = tiles.scan
                o_f = hgrn_scan(proj, _row(lb_all[i, 0]), False, tl)
                y = hgrn_scan(proj, _row(lb_all[i, 1]), True, tl, o_fwd=o_f, o_norm_g=_row(hgrn_o_norm_g[j]))
                x = matmul_residual(y.reshape(t, d), hgrn_w_o_b[j], x.reshape(t, d), tm).reshape(b, l, d)
            else:
                u, w = fnet_channel(x.reshape(t, d), gmix, cn, sn, tm)
                a_re, a_im = fnet_stage1(m1, u.reshape(b, l, d), w.reshape(b, l, d), l1, l2,
                                         tiles.dft_slabs1, tiles.dft_cols1)
                x = fnet_stage2(a_re, a_im, tw_c, tw_s, m2, x, fnet_w_o_b[j], l1, l2, tiles.dft_slabs2)
            mk, mv = mem_kv(mem, _row(norm_mem_g[i]), mem_w_kv_b[i], _row(mem_k_norm_g[i]))
            x = mem_xattn(x, _row(norm_xq_g[i]), mem_w_q_b[i], _row(mem_qg_scaled[i]), mk, mv, mem_w_o_b[i], tm)
            x = mlp(x.reshape(t, d), _row(norm_mlp_g[i]), mlp_w1_b[i], mlp_w2_b[i], tm, tiles.ff).reshape(b, l, d)
        return x

    return (run_trunk(x_prompt, mem_prompt), run_trunk(x_sample, mem_sample))
```

```python
import functools
from typing import NamedTuple

import jax
import jax.numpy as jnp
from jax import lax
from jax.experimental import pallas as pl
from jax.experimental.pallas import tpu as pltpu

F32 = jnp.float32
BF16 = jnp.bfloat16

EPS = 1e-6
D_MODEL = 2048
D_FF = 4 * D_MODEL

MLA_HEADS = 16
Q_LORA_RANK = 512
KV_LORA_RANK = 512
QK_NOPE_DIM = 128
QK_ROPE_DIM = 64
QK_HEAD_DIM = QK_NOPE_DIM + QK_ROPE_DIM
V_HEAD_DIM = 128
ROPE_THETA = 10000.0
QK_PAD_DIM = 256
SCORE_LOG2_LIMIT = 100.0
ONES_ROWS = 16

HGRN_HEADS = 16
HGRN_KEY_DIM = 128
HGRN_VAL_DIM = D_MODEL // HGRN_HEADS
HGRN_KEY_WIDTH = HGRN_HEADS * HGRN_KEY_DIM
HGRN_CHUNK = 64
HGRN_SUB = 8
LOG2_E = 1.4426950408889634

FNET_GROUPS = 4
FNET_GROUP_DIM = D_MODEL // FNET_GROUPS

MEM_HEADS = 4
MEM_HEAD_DIM = 128
MEM_WIDTH = MEM_HEADS * MEM_HEAD_DIM

LANE = 128
VMEM_LIMIT_BYTES = 56 * 1024 * 1024


def _params(*semantics):
    return pltpu.CompilerParams(dimension_semantics=semantics, vmem_limit_bytes=VMEM_LIMIT_BYTES)


def _resident(shape):
    nd = len(shape)
    return pl.BlockSpec(shape, lambda *_: (0,) * nd, pipeline_mode=pl.Buffered(1))


def _rms(x, g):
    ms = jnp.mean(x * x, axis=-1, keepdims=True)
    return x * lax.rsqrt(ms + EPS) * g


def _dot(a, b):
    return jnp.dot(a, b, preferred_element_type=F32)


def _dot_nt(a, b):
    return lax.dot_general(a, b, (((1,), (1,)), ((), ())), preferred_element_type=F32)


def _norm_matmul_kernel(x_ref, g_ref, w_ref, o_ref, xn_ref):
    @pl.when(pl.program_id(1) == 0)
    def _():
        xn_ref[...] = _rms(x_ref[...], g_ref[...]).astype(BF16)

    res = _dot(xn_ref[...], w_ref[...])
    for c in range(o_ref.shape[0]):
        o_ref[c] = res[:, c * LANE:(c + 1) * LANE].astype(o_ref.dtype)


def norm_matmul_colmajor(x, g, w, out_dtype, tm, tn):
    t, d = x.shape
    n = w.shape[1]
    return pl.pallas_call(
        _norm_matmul_kernel,
        grid=(t // tm, n // tn),
        in_specs=[
            pl.BlockSpec((tm, d), lambda i, j: (i, 0)),
            _resident((1, d)),
            pl.BlockSpec((d, tn), lambda i, j: (0, j)),
        ],
        out_specs=pl.BlockSpec((tn // LANE, tm, LANE), lambda i, j: (j, i, 0)),
        out_shape=jax.ShapeDtypeStruct((n // LANE, t, LANE), out_dtype),
        scratch_shapes=[pltpu.VMEM((tm, d), BF16)],
        compiler_params=_params("parallel", "arbitrary"),
        name="norm_matmul",
    )(x, g, w)


def _matmul_residual_kernel(a_ref, w_ref, r_ref, o_ref):
    if len(a_ref.shape) == 3:
        a = jnp.concatenate([a_ref[c] for c in range(a_ref.shape[0])], axis=-1)
    else:
        a = a_ref[...]
    o_ref[...] = r_ref[...] + _dot(a, w_ref[...])


def matmul_residual(a, w, res, tm):
    t = res.shape[0]
    k, n = w.shape
    if a.ndim == 3:
        a_spec = pl.BlockSpec((k // LANE, tm, LANE), lambda i: (0, i, 0))
    else:
        a_spec = pl.BlockSpec((tm, k), lambda i: (i, 0))
    return pl.pallas_call(
        _matmul_residual_kernel,
        grid=(t // tm,),
        in_specs=[
            a_spec,
            _resident((k, n)),
            pl.BlockSpec((tm, n), lambda i: (i, 0)),
        ],
        out_specs=pl.BlockSpec((tm, n), lambda i: (i, 0)),
        out_shape=jax.ShapeDtypeStruct((t, n), F32),
        compiler_params=_params("parallel"),
        name="matmul_residual",
    )(a, w, res)


def _mlp_kernel(x_ref, g_ref, w1_ref, w2_ref, o_ref, xn_ref, acc_ref):
    j = pl.program_id(1)

    @pl.when(j == 0)
    def _():
        xn_ref[...] = _rms(x_ref[...], g_ref[...]).astype(BF16)
        acc_ref[...] = jnp.zeros_like(acc_ref)

    h = jnp.maximum(_dot(xn_ref[...], w1_ref[...]), 0.0)
    acc_ref[...] += _dot((h * h).astype(BF16), w2_ref[...])

    @pl.when(j == pl.num_programs(1) - 1)
    def _():
        o_ref[...] = x_ref[...] + acc_ref[...]


def mlp(x, g, w1, w2, tm, tf):
    t, d = x.shape
    f = w1.shape[1]
    return pl.pallas_call(
        _mlp_kernel,
        grid=(t // tm, f // tf),
        in_specs=[
            pl.BlockSpec((tm, d), lambda i, j: (i, 0)),
            _resident((1, d)),
            pl.BlockSpec((d, tf), lambda i, j: (0, j)),
            pl.BlockSpec((tf, d), lambda i, j: (j, 0)),
        ],
        out_specs=pl.BlockSpec((tm, d), lambda i, j: (i, 0)),
        out_shape=jax.ShapeDtypeStruct((t, d), F32),
        scratch_shapes=[pltpu.VMEM((tm, d), BF16), pltpu.VMEM((tm, d), F32)],
        compiler_params=_params("parallel", "arbitrary"),
        name="mlp",
    )(x, g, w1, w2)


def _mem_kv_kernel(m_ref, g_ref, w_ref, kg_ref, k_out, v_out):
    mn = _rms(m_ref[...], g_ref[...]).astype(BF16)
    kv = _dot(mn, w_ref[...])
    kg = kg_ref[...]
    for h in range(MEM_HEADS):
        lo = h * MEM_HEAD_DIM
        k_out[:, lo:lo + MEM_HEAD_DIM] = _rms(kv[:, lo:lo + MEM_HEAD_DIM], kg).astype(BF16)
    v_out[...] = kv[:, MEM_WIDTH:].astype(BF16)


def mem_kv(mem, g, w_kv, k_g):
    b, m, d = mem.shape
    spec_out = pl.BlockSpec((None, m, MEM_WIDTH), lambda i: (i, 0, 0))
    return pl.pallas_call(
        _mem_kv_kernel,
        grid=(b,),
        in_specs=[
            pl.BlockSpec((None, m, d), lambda i: (i, 0, 0)),
            _resident((1, d)),
            _resident((d, 2 * MEM_WIDTH)),
            _resident((1, MEM_HEAD_DIM)),
        ],
        out_specs=[spec_out, spec_out],
        out_shape=[jax.ShapeDtypeStruct((b, m, MEM_WIDTH), BF16)] * 2,
        compiler_params=_params("parallel"),
        name="mem_kv",
    )(mem, g, w_kv, k_g)


def _mem_xattn_kernel(x_ref, g_ref, wq_ref, qg_ref, k_ref, v_ref, wo_ref, o_ref):
    x = x_ref[...]
    q = _dot(_rms(x, g_ref[...]).astype(BF16), wq_ref[...])
    qg = qg_ref[...]
    heads = []
    for h in range(MEM_HEADS):
        lo = h * MEM_HEAD_DIM
        qh = _rms(q[:, lo:lo + MEM_HEAD_DIM], qg).astype(BF16)
        s = _dot_nt(qh, k_ref[:, lo:lo + MEM_HEAD_DIM])
        e = jnp.exp(s - jnp.max(s, axis=-1, keepdims=True))
        p = e / jnp.sum(e, axis=-1, keepdims=True)
        heads.append(_dot(p.astype(BF16), v_ref[:, lo:lo + MEM_HEAD_DIM]))
    o = jnp.concatenate(heads, axis=-1).astype(BF16)
    o_ref[...] = x + _dot(o, wo_ref[...])


def mem_xattn(x, g, w_q, q_g_scaled, k, v, w_o, tm):
    b, l, d = x.shape
    m = k.shape[1]
    spec_x = pl.BlockSpec((None, tm, d), lambda i, j: (i, j, 0))
    spec_kv = pl.BlockSpec((None, m, MEM_WIDTH), lambda i, j: (i, 0, 0))
    return pl.pallas_call(
        _mem_xattn_kernel,
        grid=(b, l // tm),
        in_specs=[
            spec_x,
            _resident((1, d)),
            _resident((d, MEM_WIDTH)),
            _resident((1, MEM_HEAD_DIM)),
            spec_kv,
            spec_kv,
            _resident((MEM_WIDTH, d)),
        ],
        out_specs=spec_x,
        out_shape=jax.ShapeDtypeStruct((b, l, d), F32),
        compiler_params=_params("parallel", "parallel"),
        name="mem_xattn",
    )(x, g, w_q, q_g_scaled, k, v, w_o)


def _mla_pre_kernel(x_ref, gmix_ref, win_ref, qag_ref, kvag_ref, wqt_ref, wk_ref, wvt_ref, gqt_ref, gk_ref,
                    cosk_ref, sin_lo_ref, sin_hi_ref, cosq_ref, sinq_ref, q_out, k_out, v_out):
    tm = x_ref.shape[0]
    half = QK_ROPE_DIM // 2
    xn = _rms(x_ref[...], gmix_ref[...]).astype(BF16)
    proj = _dot(xn, win_ref[...])
    ql = _rms(proj[:, :Q_LORA_RANK], qag_ref[...])
    kvl = _rms(proj[:, Q_LORA_RANK:Q_LORA_RANK + KV_LORA_RANK], kvag_ref[...])
    kr = proj[:, Q_LORA_RANK + KV_LORA_RANK:]
    kvl_t = kvl.T.astype(BF16)
    qt = _dot(wqt_ref[...], ql.T.astype(BF16))
    vt = _dot(wvt_ref[...], kvl_t)
    kn_all = _dot(kvl.astype(BF16), wk_ref[...])
    inv_dim = 1.0 / QK_HEAD_DIM

    gk = gk_ref[...]
    ss_kr = jnp.sum(kr * kr, axis=-1, keepdims=True)
    krg = kr * gk[:, QK_NOPE_DIM:]
    kr_rot = (krg * cosk_ref[...] + pltpu.roll(krg, half, 1) * sin_hi_ref[...]
              + pltpu.roll(krg, LANE - half, 1) * sin_lo_ref[...])
    reps = tm // LANE
    gqt = jnp.tile(gqt_ref[...], (1, reps))
    cos_q = cosq_ref[...]
    sin_q = sinq_ref[...]
    zero_rows = jnp.zeros((QK_PAD_DIM - QK_HEAD_DIM, tm), BF16)
    for h in range(MLA_HEADS):
        qh = qt[h * QK_PAD_DIM:h * QK_PAD_DIM + QK_HEAD_DIM, :]
        inv_q = lax.rsqrt(jnp.sum(qh * qh, axis=0, keepdims=True) * inv_dim + EPS)
        qn = qh * inv_q * gqt
        x1 = qn[QK_NOPE_DIM:QK_NOPE_DIM + half, :]
        x2 = qn[QK_NOPE_DIM + half:, :]
        q_out[h, :QK_NOPE_DIM, :] = qn[:QK_NOPE_DIM, :].astype(BF16)
        q_out[h, QK_NOPE_DIM:QK_NOPE_DIM + half, :] = (x1 * cos_q - x2 * sin_q).astype(BF16)
        q_out[h, QK_NOPE_DIM + half:QK_HEAD_DIM, :] = (x1 * sin_q + x2 * cos_q).astype(BF16)
        q_out[h, QK_HEAD_DIM:, :] = zero_rows
        kn = kn_all[:, h * QK_NOPE_DIM:(h + 1) * QK_NOPE_DIM]
        inv_k = lax.rsqrt((jnp.sum(kn * kn, axis=-1, keepdims=True) + ss_kr) * inv_dim + EPS)
        k_out[h, :, :QK_NOPE_DIM] = (kn * inv_k * gk[:, :QK_NOPE_DIM]).astype(BF16)
        k_out[h, :, QK_NOPE_DIM:] = (kr_rot * inv_k).astype(BF16)
        v_out[h] = vt[h * V_HEAD_DIM:(h + 1) * V_HEAD_DIM, :].astype(BF16)


def mla_pre(x, gmix, w_in_p, qag, kvag, w_qt, w_k, w_vt, gq_t, gk_p, rope_tabs, tm):
    b, l, d = x.shape
    h = MLA_HEADS
    cos_k, sin_lo, sin_hi, cos_q, sin_q = rope_tabs
    spec_tab_k = pl.BlockSpec((tm, LANE), lambda i, j: (j, 0))
    spec_tab_q = pl.BlockSpec((QK_ROPE_DIM // 2, tm), lambda i, j: (0, j))
    spec_qv = lambda rows: pl.BlockSpec((None, h, rows, tm), lambda i, j: (i, 0, 0, j))
    return pl.pallas_call(
        _mla_pre_kernel,
        grid=(b, l // tm),
        in_specs=[
            pl.BlockSpec((None, tm, d), lambda i, j: (i, j, 0)),
            _resident((1, d)),
            _resident(w_in_p.shape),
            _resident((1, Q_LORA_RANK)),
            _resident((1, KV_LORA_RANK)),
            _resident(w_qt.shape),
            _resident(w_k.shape),
            _resident(w_vt.shape),
            _resident(gq_t.shape),
            _resident((1, QK_PAD_DIM)),
            spec_tab_k, spec_tab_k, spec_tab_k, spec_tab_q, spec_tab_q,
        ],
        out_specs=[spec_qv(QK_PAD_DIM), pl.BlockSpec((None, h, tm, QK_PAD_DIM), lambda i, j: (i, 0, j, 0)),
                   spec_qv(V_HEAD_DIM)],
        out_shape=[
            jax.ShapeDtypeStruct((b, h, QK_PAD_DIM, l), BF16),
            jax.ShapeDtypeStruct((b, h, l, QK_PAD_DIM), BF16),
            jax.ShapeDtypeStruct((b, h, V_HEAD_DIM, l), BF16),
        ],
        compiler_params=_params("parallel", "parallel"),
        name="mla_pre",
    )(x, gmix, w_in_p, qag, kvag, w_qt, w_k, w_vt, gq_t, gk_p, cos_k, sin_lo, sin_hi, cos_q, sin_q)


def _flash_kernel(q_ref, k_ref, vt_ref, o_ref, m_ref, l_ref, acc_ref):
    kv = pl.program_id(3)

    @pl.when(kv == 0)
    def _():
        m_ref[...] = jnp.full_like(m_ref, -jnp.inf)
        l_ref[...] = jnp.zeros_like(l_ref)
        acc_ref[...] = jnp.zeros_like(acc_ref)

    s = _dot(k_ref[...], q_ref[...])
    m_prev = m_ref[...]
    m_new = jnp.maximum(m_prev, jnp.max(s, axis=0, keepdims=True))
    alpha = jnp.exp2(m_prev - m_new)
    p = jnp.exp2(s - m_new)
    l_ref[...] = alpha * l_ref[...] + jnp.sum(p, axis=0, keepdims=True)
    acc_ref[...] = alpha * acc_ref[...] + _dot(vt_ref[...], p.astype(BF16))
    m_ref[...] = m_new

    @pl.when(kv == pl.num_programs(3) - 1)
    def _():
        o_ref[...] = (acc_ref[...] / l_ref[...]).T.astype(o_ref.dtype)


def flash_attention(qt, k, vt, tq, tk):
    b, h, dk, l = qt.shape
    dv = vt.shape[2]
    return pl.pallas_call(
        _flash_kernel,
        grid=(b, h, l // tq, l // tk),
        in_specs=[
            pl.BlockSpec((None, None, dk, tq), lambda i, j, a, c: (i, j, 0, a)),
            pl.BlockSpec((None, None, tk, dk), lambda i, j, a, c: (i, j, c, 0)),
            pl.BlockSpec((None, None, dv, tk), lambda i, j, a, c: (i, j, 0, c)),
        ],
        out_specs=pl.BlockSpec((None, tq, dv), lambda i, j, a, c: (i, a, j)),
        out_shape=jax.ShapeDtypeStruct((b, l, h * dv), BF16),
        scratch_shapes=[pltpu.VMEM((1, tq), F32), pltpu.VMEM((1, tq), F32), pltpu.VMEM((dv, tq), F32)],
        compiler_params=_params("parallel", "parallel", "parallel", "arbitrary"),
        name="flash_attention",
    )(qt, k, vt)


def _flash_bounded_kernel(q_ref, k_ref, vt_ref, o_ref, acc_ref, *, tk):
    dv = vt_ref.shape[0]
    q = q_ref[...]
    ones = jnp.ones((ONES_ROWS, tk), BF16)
    for c in range(k_ref.shape[0] // tk):
        cols = slice(c * tk, (c + 1) * tk)
        p = jnp.exp2(_dot(k_ref[cols, :], q)).astype(BF16)
        pv = _dot(jnp.concatenate([vt_ref[:, cols], ones], axis=0), p)
        if c == 0:
            acc_ref[...] = pv
        else:
            acc_ref[...] += pv
    acc = acc_ref[...]
    o_ref[...] = (acc[:dv] / acc[dv:dv + 1]).T.astype(o_ref.dtype)


def flash_attention_bounded(qt, k, vt, tq, tk):
    b, h, dk, l = qt.shape
    dv = vt.shape[2]
    return pl.pallas_call(
        functools.partial(_flash_bounded_kernel, tk=tk),
        grid=(b, h, l // tq),
        in_specs=[
            pl.BlockSpec((None, None, dk, tq), lambda i, j, a: (i, j, 0, a)),
            pl.BlockSpec((None, None, l, dk), lambda i, j, a: (i, j, 0, 0)),
            pl.BlockSpec((None, None, dv, l), lambda i, j, a: (i, j, 0, 0)),
        ],
        out_specs=pl.BlockSpec((None, tq, dv), lambda i, j, a: (i, a, j)),
        out_shape=jax.ShapeDtypeStruct((b, l, h * dv), BF16),
        scratch_shapes=[pltpu.VMEM((dv + ONES_ROWS, tq), F32)],
        compiler_params=_params("parallel", "parallel", "arbitrary"),
        name="flash_attention_bounded",
    )(qt, k, vt)


def _hgrn_scan_kernel(*refs, reverse, n_chunks):
    if reverse:
        q_ref, z_ref, v_ref, lb_ref, sel_ref, of_ref, gate_ref, ong_ref, o_ref, s_ref, b_ref, kk_ref, lhs_ref = refs
    else:
        q_ref, z_ref, v_ref, lb_ref, sel_ref, o_ref, s_ref, b_ref, kk_ref, lhs_ref = refs
    c = HGRN_CHUNK
    sub = HGRN_SUB
    per = c // sub
    kd = HGRN_KEY_DIM
    tl = n_chunks * c

    @pl.when(pl.program_id(2) == 0)
    def _():
        s_ref[...] = jnp.zeros_like(s_ref)

    lb = lb_ref[...]
    z = z_ref[...]
    e = jnp.exp(-jnp.abs(z))
    r = 1.0 / (1.0 + e)
    f = lb + (1.0 - lb) * (jnp.where(z >= 0.0, 1.0, e) * r)
    log2_f = jnp.where(f > 0.0, jnp.log2(f), (z + jnp.log1p(-lb)) * LOG2_E)
    kk = (1.0 - lb) * (jnp.where(z >= 0.0, e, 1.0) * r)

    row = lax.broadcasted_iota(jnp.int32, (c, c), 0)
    col = lax.broadcasted_iota(jnp.int32, (c, c), 1)
    visible = (col >= row) if reverse else (col <= row)
    tri_f = visible.astype(F32)
    for ci in range(n_chunks):
        bsum = jnp.dot(tri_f, log2_f[ci * c:(ci + 1) * c], precision=lax.Precision.HIGHEST,
                       preferred_element_type=F32)
        b_ref[ci * per:(ci + 1) * per] = bsum.reshape(per, sub, kd)
    kk_ref[...] = kk.reshape(tl // sub, sub, kd)

    q3 = q_ref[...].reshape(tl // sub, sub, kd)
    b3 = b_ref[...]
    for s in range(sub):
        k_row = kk_ref[:, s:s + 1, :]
        b_row = b_ref[:, s:s + 1, :]
        w = q3 * k_row * jnp.exp2(jnp.minimum(b3 - b_row, 0.0))
        lhs_ref[:, s * kd:(s + 1) * kd] = w.reshape(tl, kd).astype(BF16)
    d_all = _dot(lhs_ref[...], sel_ref[...])

    q2 = q_ref[...]
    k2 = kk.reshape(tl, kd)
    b2 = b3.reshape(tl, kd)

    split_q, split_k, split_mask = [], [], []
    group = c
    while group > sub:
        half = group // 2
        pos = lax.broadcasted_iota(jnp.int32, (1, group, 1), 1)
        is_q = (pos < half) if reverse else (pos >= half)
        bg = b2.reshape(tl // group, group, kd)
        split_row = half if reverse else half - 1
        b_split = bg[:, split_row:split_row + 1, :]
        dec = jnp.exp2(jnp.where(is_q, bg - b_split, b_split - bg))
        split_q.append(jnp.where(is_q, q2.reshape(bg.shape) * dec, 0.0).astype(BF16).reshape(tl, kd))
        split_k.append(jnp.where(is_q, 0.0, k2.reshape(bg.shape) * dec).astype(BF16).reshape(tl, kd))
        split_mask.append(None if group == c else jnp.where((row // group) == (col // group), 1.0, 0.0))
        group = half
    mask_d = jnp.where(((row // sub) == (col // sub)) & visible, 1.0, 0.0)

    edge = 0 if reverse else c - 1
    bc3 = b2.reshape(n_chunks, c, kd)
    b_edge = bc3[:, edge:edge + 1, :]
    q_in = (q2.reshape(bc3.shape) * jnp.exp2(bc3)).astype(BF16)
    k_out = (k2.reshape(bc3.shape) * jnp.exp2(b_edge - bc3)).astype(BF16)
    edge_decay = jnp.exp2(b_edge)

    order = range(n_chunks - 1, -1, -1) if reverse else range(n_chunks)
    o_intra, s_add = {}, {}
    for ci in order:
        rows = slice(ci * c, (ci + 1) * c)
        att = d_all[rows, :] * mask_d
        for qs, ks, mask in zip(split_q, split_k, split_mask):
            part = _dot_nt(qs[rows, :], ks[rows, :])
            att = att + (part if mask is None else part * mask)
        vc = v_ref[rows, :]
        o_intra[ci] = _dot(att.astype(BF16), vc.astype(BF16))
        s_add[ci] = _dot(vc.T.astype(BF16), k_out[ci])

    st = s_ref[...]
    for ci in order:
        rows = slice(ci * c, (ci + 1) * c)
        o = o_intra[ci] + _dot_nt(q_in[ci], st.astype(BF16))
        st = st * edge_decay[ci] + s_add[ci]
        if reverse:
            o = _rms(o + of_ref[rows, :], ong_ref[...])
            g = gate_ref[rows, :]
            o_ref[rows, :] = (o * (g * jax.nn.sigmoid(g))).astype(o_ref.dtype)
        else:
            o_ref[rows, :] = o
    s_ref[...] = st


def hgrn_scan(proj, lb, reverse, b, l, tl, o_fwd=None, o_norm_g=None):
    h = HGRN_HEADS
    nl = l // tl
    n_chunks = tl // HGRN_CHUNK
    kd = HGRN_KEY_DIM

    def pos(i, j):
        return i * nl + ((nl - 1 - j) if reverse else j)

    def col_spec(section):
        return pl.BlockSpec((None, tl, kd), lambda i, hh, j: (section * h + hh, pos(i, j), 0))

    sel_s = lax.broadcasted_iota(jnp.int32, (HGRN_SUB, kd, HGRN_CHUNK), 0)
    sel_j = lax.broadcasted_iota(jnp.int32, (HGRN_SUB, kd, HGRN_CHUNK), 2)
    sel = (sel_j % HGRN_SUB == sel_s).astype(BF16).reshape(HGRN_SUB * kd, HGRN_CHUNK)

    in_specs = [col_spec(0), col_spec(2 if reverse else 1), col_spec(3),
                pl.BlockSpec((1, kd), lambda i, hh, j: (0, hh)), _resident(sel.shape)]
    args = [proj, proj, proj, lb, sel]
    out_spec = pl.BlockSpec((None, tl, kd), lambda i, hh, j: (hh, pos(i, j), 0))
    if reverse:
        in_specs += [out_spec, col_spec(4), _resident((1, HGRN_VAL_DIM))]
        args += [o_fwd, proj, o_norm_g]
    return pl.pallas_call(
        functools.partial(_hgrn_scan_kernel, reverse=reverse, n_chunks=n_chunks),
        grid=(b, h, nl),
        in_specs=in_specs,
        out_specs=out_spec,
        out_shape=jax.ShapeDtypeStruct((h, b * l, kd), BF16 if reverse else F32),
        scratch_shapes=[
            pltpu.VMEM((HGRN_VAL_DIM, kd), F32),
            pltpu.VMEM((tl // HGRN_SUB, HGRN_SUB, kd), F32),
            pltpu.VMEM((tl // HGRN_SUB, HGRN_SUB, kd), F32),
            pltpu.VMEM((tl, HGRN_SUB * kd), BF16),
        ],
        compiler_params=_params("parallel", "parallel", "arbitrary"),
        name="hgrn_scan_bwd" if reverse else "hgrn_scan_fwd",
    )(*args)


def _fnet_channel_kernel(x_ref, g_ref, cn_ref, sn_ref, u_ref, w_ref):
    xn = _rms(x_ref[...], g_ref[...]).astype(BF16)
    for gi in range(FNET_GROUPS):
        cols = slice(gi * FNET_GROUP_DIM, (gi + 1) * FNET_GROUP_DIM)
        u_ref[:, cols] = _dot(xn[:, cols], cn_ref[...]).astype(BF16)
        w_ref[:, cols] = _dot(xn[:, cols], sn_ref[...]).astype(BF16)


def fnet_channel(x, g, cn, sn, tm):
    t, d = x.shape
    spec = pl.BlockSpec((tm, d), lambda i: (i, 0))
    return pl.pallas_call(
        _fnet_channel_kernel,
        grid=(t // tm,),
        in_specs=[spec, _resident((1, d)), _resident(cn.shape), _resident(sn.shape)],
        out_specs=[spec, spec],
        out_shape=[jax.ShapeDtypeStruct((t, d), BF16)] * 2,
        compiler_params=_params("parallel"),
        name="fnet_channel",
    )(x, g, cn, sn)


def _fnet_stage1_kernel(m1_ref, u_ref, w_ref, are_ref, aim_ref, uf_ref, wf_ref, ar_ref, ai_ref):
    l1, n_slabs, _ = u_ref.shape
    m1 = m1_ref[...]
    uf_ref[...] = u_ref[...].astype(F32)
    wf_ref[...] = w_ref[...].astype(F32)
    for s in range(n_slabs):
        z = jnp.concatenate([uf_ref[:, s, :], wf_ref[:, s, :]], axis=0).astype(BF16)
        a = _dot(m1, z)
        ar_ref[:, s, :] = a[:l1]
        ai_ref[:, s, :] = a[l1:]
    are_ref[...] = ar_ref[...].astype(BF16)
    aim_ref[...] = ai_ref[...].astype(BF16)


def fnet_stage1(m1, u, w, l1, l2, n_slabs, cb):
    b, l, d = u.shape
    spec = pl.BlockSpec((None, l1, n_slabs, cb), lambda i, j, c: (i, 0, j, c))
    a_re, a_im = pl.pallas_call(
        _fnet_stage1_kernel,
        grid=(b, l2 // n_slabs, d // cb),
        in_specs=[_resident(m1.shape), spec, spec],
        out_specs=[spec, spec],
        out_shape=[jax.ShapeDtypeStruct((b, l1, l2, d), BF16)] * 2,
        scratch_shapes=[pltpu.VMEM((l1, n_slabs, cb), F32)] * 4,
        compiler_params=_params("parallel", "parallel", "parallel"),
        name="fnet_stage1",
    )(m1, u.reshape(b, l1, l2, d), w.reshape(b, l1, l2, d))
    return a_re, a_im


def _fnet_stage2_kernel(are_ref, aim_ref, twc_ref, tws_ref, m2_ref, x_ref, wo_ref, o_ref, y_ref):
    n_slabs, l2, d = are_ref.shape
    m2 = m2_ref[...]
    for j in range(n_slabs):
        a_re = are_ref[j].astype(F32)
        a_im = aim_ref[j].astype(F32)
        c = jnp.tile(twc_ref[j], (1, d // LANE))
        s = jnp.tile(tws_ref[j], (1, d // LANE))
        b_re = a_re * c + a_im * s
        b_im = a_im * c - a_re * s
        y = _dot(m2, jnp.concatenate([b_re, b_im], axis=0).astype(BF16))
        y_ref[j * l2:(j + 1) * l2, :] = y.astype(BF16)
    out = _dot(y_ref[...], wo_ref[...])
    for j in range(n_slabs):
        o_ref[:, j, :] = x_ref[:, j, :] + out[j * l2:(j + 1) * l2, :]


def fnet_stage2(a_re, a_im, tw_c, tw_s, m2, x, w_o, l1, l2, n_slabs):
    b, l, d = x.shape
    spec_a = pl.BlockSpec((None, n_slabs, l2, d), lambda i, j: (i, j, 0, 0))
    spec_tw = pl.BlockSpec((n_slabs, l2, LANE), lambda i, j: (j, 0, 0))
    spec_x = pl.BlockSpec((None, l2, n_slabs, d), lambda i, j: (i, 0, j, 0))
    out = pl.pallas_call(
        _fnet_stage2_kernel,
        grid=(b, l1 // n_slabs),
        in_specs=[spec_a, spec_a, spec_tw, spec_tw, _resident(m2.shape), spec_x, _resident(w_o.shape)],
        out_specs=spec_x,
        out_shape=jax.ShapeDtypeStruct((b, l2, l1, d), F32),
        scratch_shapes=[pltpu.VMEM((n_slabs * l2, d), BF16)],
        compiler_params=_params("parallel", "parallel"),
        name="fnet_stage2",
    )(a_re, a_im, tw_c, tw_s, m2, x.reshape(b, l2, l1, d), w_o)
    return out.reshape(b, l, d)


def _fnet_seq_tables(l1, l2):
    c1, s1 = _dft_tables(l1)
    c2, s2 = _dft_tables(l2)
    m1 = jnp.concatenate([jnp.concatenate([c1, -s1], axis=1), jnp.concatenate([-s1, -c1], axis=1)], axis=0)
    m2 = jnp.concatenate([c2, s2], axis=1)
    k1 = jnp.arange(l1, dtype=jnp.int32)[:, None]
    n2 = jnp.arange(l2, dtype=jnp.int32)[None, :]
    ang = ((k1 * n2) % (l1 * l2)).astype(F32)[:, :, None] * (2.0 * jnp.pi / (l1 * l2))
    tw_c = jnp.broadcast_to(jnp.cos(ang), (l1, l2, LANE))
    tw_s = jnp.broadcast_to(jnp.sin(ang), (l1, l2, LANE))
    return m1.astype(BF16), m2.astype(BF16), tw_c, tw_s


def _dft_tables(n):
    idx = jnp.arange(n, dtype=jnp.int32)
    jk = (idx[:, None] * idx[None, :]) % n
    ang = jk.astype(F32) * (2.0 * jnp.pi / n)
    scale = n ** -0.5
    return jnp.cos(ang) * scale, jnp.sin(ang) * scale


def _rope_tables(length):
    dim = QK_ROPE_DIM
    half = dim // 2
    inv = 1.0 / (ROPE_THETA ** (jnp.arange(0, dim, 2, dtype=F32) / dim))
    ang = jnp.arange(length, dtype=F32)[:, None] * inv[None, :]
    cos, sin = jnp.cos(ang), jnp.sin(ang)
    zero = jnp.zeros_like(cos)
    cos_t = jnp.concatenate([cos, cos, zero, zero], axis=-1)
    sin_lo = jnp.concatenate([-sin, zero, zero, zero], axis=-1)
    sin_hi = jnp.concatenate([zero, sin, zero, zero], axis=-1)
    return cos_t, sin_lo, sin_hi, cos.T, sin.T


def _pad_cols(a, n):
    return jnp.concatenate([a, jnp.zeros(a.shape[:-1] + (n,), a.dtype)], axis=-1)


def _row(v):
    return v.reshape(1, -1).astype(F32)


class _Tiles(NamedTuple):
    token: int
    mla_pre: int
    attn_q: int
    attn_k: int
    scan: int
    hgrn_cols: int
    dft_inner: int
    dft_slabs1: int
    dft_cols1: int
    dft_slabs2: int
    ff: int


def _tiles(l):
    dft_inner = 64 if l >= 1024 else 16
    return _Tiles(token=min(l, 512), mla_pre=min(l, 256), attn_q=min(l, 2048), attn_k=min(l, 1024),
                  scan=min(l, 2048),
                  hgrn_cols=2048, dft_inner=dft_inner, dft_slabs1=16, dft_cols1=1024,
                  dft_slabs2=min(8, l // dft_inner), ff=1024)


def kernel(x_prompt, x_sample, mem_prompt, mem_sample, norm_mix_g, mla_w_in, mla_q_a_norm_g, mla_kv_a_norm_g, mla_w_q_b, mla_w_kv_b, mla_q_norm_g, mla_k_norm_g, mla_w_o, hgrn_w_in, hgrn_lb_logits, hgrn_o_norm_g, hgrn_w_o, fnet_w_o, norm_xq_g, norm_mem_g, mem_w_q, mem_w_kv, mem_q_norm_g, mem_k_norm_g, mem_w_o, norm_mlp_g, mlp_w1, mlp_w2):
    depth = norm_mix_g.shape[0]
    n_mixers = 3

    lb_all = jnp.cumsum(jax.nn.softmax(hgrn_lb_logits.astype(F32), axis=0), axis=0)
    lb_all = lb_all - lb_all[:1]

    mla_w_in_p = _pad_cols(mla_w_in, LANE - QK_ROPE_DIM).astype(BF16)
    n_mla = mla_w_in.shape[0]
    wqb = mla_w_q_b.reshape(n_mla, Q_LORA_RANK, MLA_HEADS, QK_HEAD_DIM)
    mla_w_qb_p = _pad_cols(wqb, QK_PAD_DIM - QK_HEAD_DIM).reshape(n_mla, Q_LORA_RANK, MLA_HEADS * QK_PAD_DIM).astype(BF16)
    mla_w_qt = jnp.swapaxes(mla_w_qb_p, 1, 2)
    kvb = mla_w_kv_b.reshape(n_mla, KV_LORA_RANK, MLA_HEADS, QK_NOPE_DIM + V_HEAD_DIM)
    mla_w_k = kvb[..., :QK_NOPE_DIM].reshape(n_mla, KV_LORA_RANK, MLA_HEADS * QK_NOPE_DIM).astype(BF16)
    mla_w_vt = jnp.swapaxes(kvb[..., QK_NOPE_DIM:].reshape(n_mla, KV_LORA_RANK, MLA_HEADS * V_HEAD_DIM),
                            1, 2).astype(BF16)
    mla_gq_t = jnp.broadcast_to((mla_q_norm_g * (QK_HEAD_DIM ** -0.5 * LOG2_E))[:, :, None],
                                (n_mla, QK_HEAD_DIM, LANE))
    mla_score_bound = (QK_HEAD_DIM ** 0.5 * LOG2_E * 1.02) * (
        jnp.max(jnp.abs(mla_q_norm_g), axis=-1) * jnp.max(jnp.abs(mla_k_norm_g), axis=-1))
    mla_gk_p = _pad_cols(mla_k_norm_g, QK_PAD_DIM - QK_HEAD_DIM)
    mla_w_o_b = mla_w_o.astype(BF16)
    hgrn_w_in_b = hgrn_w_in.astype(BF16)
    hgrn_w_o_b = hgrn_w_o.astype(BF16)
    fnet_w_o_b = fnet_w_o.astype(BF16)
    mem_w_q_b = mem_w_q.astype(BF16)
    mem_w_kv_b = mem_w_kv.astype(BF16)
    mem_w_o_b = mem_w_o.astype(BF16)
    mem_qg_scaled = mem_q_norm_g * (MEM_HEAD_DIM ** -0.5)
    mlp_w1_b = mlp_w1.astype(BF16)
    mlp_w2_b = mlp_w2.astype(BF16)

    has_fnet = depth >= n_mixers
    if has_fnet:
        cn, sn = _dft_tables(FNET_GROUP_DIM)
        cn, sn = cn.astype(BF16), sn.astype(BF16)

    def run_trunk(x, mem):
        b, l, d = x.shape
        t = b * l
        tiles = _tiles(l)
        tm = tiles.token
        if depth >= 1:
            rope_tabs = _rope_tables(l)
        if has_fnet:
            l2 = tiles.dft_inner
            l1 = l // l2
            m1, m2, tw_c, tw_s = _fnet_seq_tables(l1, l2)
        for i in range(depth):
            kind = i % n_mixers
            j = i // n_mixers
            gmix = _row(norm_mix_g[i])
            if kind == 0:
                q, k, v = mla_pre(x, gmix, mla_w_in_p[j], _row(mla_q_a_norm_g[j]), _row(mla_kv_a_norm_g[j]),
                                  mla_w_qt[j], mla_w_k[j], mla_w_vt[j], mla_gq_t[j], _row(mla_gk_p[j]),
                                  rope_tabs, tm=tiles.mla_pre)
                o = lax.cond(mla_score_bound[j] <= SCORE_LOG2_LIMIT,
                             functools.partial(flash_attention_bounded, tq=tiles.attn_q, tk=tiles.attn_k),
                             functools.partial(flash_attention, tq=tiles.attn_k, tk=tiles.attn_k), q, k, v)
                x = matmul_residual(o.reshape(t, d), mla_w_o_b[j], x.reshape(t, d), tm).reshape(b, l, d)
            elif kind == 1:
                proj = norm_matmul_colmajor(x.reshape(t, d), gmix, hgrn_w_in_b[j], F32, tm, tiles.hgrn_cols)
                tl = tiles.scan
                o_f = hgrn_scan(proj, _row(lb_all[i, 0]), False, b, l, tl)
                y = hgrn_scan(proj, _row(lb_all[i, 1]), True, b, l, tl, o_fwd=o_f,
                              o_norm_g=_row(hgrn_o_norm_g[j]))
                x = matmul_residual(y, hgrn_w_o_b[j], x.reshape(t, d), tm).reshape(b, l, d)
            else:
                u, w = fnet_channel(x.reshape(t, d), gmix, cn, sn, tm)
                a_re, a_im = fnet_stage1(m1, u.reshape(b, l, d), w.reshape(b, l, d), l1, l2,
                                         tiles.dft_slabs1, tiles.dft_cols1)
                x = fnet_stage2(a_re, a_im, tw_c, tw_s, m2, x, fnet_w_o_b[j], l1, l2, tiles.dft_slabs2)
            mk, mv = mem_kv(mem, _row(norm_mem_g[i]), mem_w_kv_b[i], _row(mem_k_norm_g[i]))
            x = mem_xattn(x, _row(norm_xq_g[i]), mem_w_q_b[i], _row(mem_qg_scaled[i]), mk, mv, mem_w_o_b[i], tm)
            x = mlp(x.reshape(t, d), _row(norm_mlp_g[i]), mlp_w1_b[i], mlp_w2_b[i], tm, tiles.ff).reshape(b, l, d)
        return x

    return (run_trunk(x_prompt, mem_prompt), run_trunk(x_sample, mem_sample))
```

```python
import functools
from typing import NamedTuple

import jax
import jax.numpy as jnp
from jax import lax
from jax.experimental import pallas as pl
from jax.experimental.pallas import tpu as pltpu

F32 = jnp.float32
BF16 = jnp.bfloat16

EPS = 1e-6
D_MODEL = 2048
D_FF = 4 * D_MODEL

MLA_HEADS = 16
Q_LORA_RANK = 512
KV_LORA_RANK = 512
QK_NOPE_DIM = 128
QK_ROPE_DIM = 64
QK_HEAD_DIM = QK_NOPE_DIM + QK_ROPE_DIM
V_HEAD_DIM = 128
ROPE_THETA = 10000.0
QK_PAD_DIM = 256
SCORE_LOG2_LIMIT = 100.0
ONES_ROWS = 16

HGRN_HEADS = 16
HGRN_KEY_DIM = 128
HGRN_VAL_DIM = D_MODEL // HGRN_HEADS
HGRN_KEY_WIDTH = HGRN_HEADS * HGRN_KEY_DIM
HGRN_CHUNK = 64
HGRN_SUB = 8
LOG2_E = 1.4426950408889634

FNET_GROUPS = 4
FNET_GROUP_DIM = D_MODEL // FNET_GROUPS

MEM_HEADS = 4
MEM_HEAD_DIM = 128
MEM_WIDTH = MEM_HEADS * MEM_HEAD_DIM

LANE = 128
VMEM_LIMIT_BYTES = 56 * 1024 * 1024


def _params(*semantics):
    return pltpu.CompilerParams(dimension_semantics=semantics, vmem_limit_bytes=VMEM_LIMIT_BYTES)


def _resident(shape):
    nd = len(shape)
    return pl.BlockSpec(shape, lambda *_: (0,) * nd, pipeline_mode=pl.Buffered(1))


def _rms(x, g):
    ms = jnp.mean(x * x, axis=-1, keepdims=True)
    return x * lax.rsqrt(ms + EPS) * g


def _dot(a, b):
    return jnp.dot(a, b, preferred_element_type=F32)


def _dot_nt(a, b):
    return lax.dot_general(a, b, (((1,), (1,)), ((), ())), preferred_element_type=F32)


def _norm_matmul_kernel(x_ref, g_ref, w_ref, o_ref, xn_ref):
    @pl.when(pl.program_id(1) == 0)
    def _():
        xn_ref[...] = _rms(x_ref[...], g_ref[...]).astype(BF16)

    res = _dot(xn_ref[...], w_ref[...])
    for c in range(o_ref.shape[0]):
        o_ref[c] = res[:, c * LANE:(c + 1) * LANE].astype(o_ref.dtype)


def norm_matmul_colmajor(x, g, w, out_dtype, tm, tn):
    t, d = x.shape
    n = w.shape[1]
    return pl.pallas_call(
        _norm_matmul_kernel,
        grid=(t // tm, n // tn),
        in_specs=[
            pl.BlockSpec((tm, d), lambda i, j: (i, 0)),
            _resident((1, d)),
            pl.BlockSpec((d, tn), lambda i, j: (0, j)),
        ],
        out_specs=pl.BlockSpec((tn // LANE, tm, LANE), lambda i, j: (j, i, 0)),
        out_shape=jax.ShapeDtypeStruct((n // LANE, t, LANE), out_dtype),
        scratch_shapes=[pltpu.VMEM((tm, d), BF16)],
        compiler_params=_params("parallel", "arbitrary"),
        name="norm_matmul",
    )(x, g, w)


def _matmul_residual_kernel(a_ref, w_ref, r_ref, o_ref):
    if len(a_ref.shape) == 3:
        a = jnp.concatenate([a_ref[c] for c in range(a_ref.shape[0])], axis=-1)
    else:
        a = a_ref[...]
    o_ref[...] = r_ref[...] + _dot(a, w_ref[...])


def matmul_residual(a, w, res, tm):
    t = res.shape[0]
    k, n = w.shape
    if a.ndim == 3:
        a_spec = pl.BlockSpec((k // LANE, tm, LANE), lambda i: (0, i, 0))
    else:
        a_spec = pl.BlockSpec((tm, k), lambda i: (i, 0))
    return pl.pallas_call(
        _matmul_residual_kernel,
        grid=(t // tm,),
        in_specs=[
            a_spec,
            _resident((k, n)),
            pl.BlockSpec((tm, n), lambda i: (i, 0)),
        ],
        out_specs=pl.BlockSpec((tm, n), lambda i: (i, 0)),
        out_shape=jax.ShapeDtypeStruct((t, n), F32),
        compiler_params=_params("parallel"),
        name="matmul_residual",
    )(a, w, res)


def _mlp_kernel(x_ref, xn_ref, w1_ref, w2_ref, o_ref, acc_ref):
    j = pl.program_id(1)

    @pl.when(j == 0)
    def _():
        acc_ref[...] = x_ref[...]

    h = jnp.maximum(_dot(xn_ref[...], w1_ref[...]), 0.0)
    acc_ref[...] += _dot((h * h).astype(BF16), w2_ref[...])

    @pl.when(j == pl.num_programs(1) - 1)
    def _():
        o_ref[...] = acc_ref[...]


def mlp(x, xn, w1, w2, tm, tf):
    t, d = x.shape
    f = w1.shape[1]
    spec = pl.BlockSpec((tm, d), lambda i, j: (i, 0))
    return pl.pallas_call(
        _mlp_kernel,
        grid=(t // tm, f // tf),
        in_specs=[
            spec,
            spec,
            pl.BlockSpec((d, tf), lambda i, j: (0, j)),
            pl.BlockSpec((tf, d), lambda i, j: (j, 0)),
        ],
        out_specs=spec,
        out_shape=jax.ShapeDtypeStruct((t, d), F32),
        scratch_shapes=[pltpu.VMEM((tm, d), F32)],
        compiler_params=_params("parallel", "arbitrary"),
        name="mlp",
    )(x, xn, w1, w2)


def _mem_kv_kernel(m_ref, g_ref, w_ref, kg_ref, k_out, v_out):
    mn = _rms(m_ref[...], g_ref[...]).astype(BF16)
    kv = _dot(mn, w_ref[...])
    kg = kg_ref[...]
    for h in range(MEM_HEADS):
        lo = h * MEM_HEAD_DIM
        k_out[:, lo:lo + MEM_HEAD_DIM] = _rms(kv[:, lo:lo + MEM_HEAD_DIM], kg).astype(BF16)
    v_out[...] = kv[:, MEM_WIDTH:].astype(BF16)


def mem_kv(mem, g, w_kv, k_g):
    b, m, d = mem.shape
    spec_out = pl.BlockSpec((None, m, MEM_WIDTH), lambda i: (i, 0, 0))
    return pl.pallas_call(
        _mem_kv_kernel,
        grid=(b,),
        in_specs=[
            pl.BlockSpec((None, m, d), lambda i: (i, 0, 0)),
            _resident((1, d)),
            _resident((d, 2 * MEM_WIDTH)),
            _resident((1, MEM_HEAD_DIM)),
        ],
        out_specs=[spec_out, spec_out],
        out_shape=[jax.ShapeDtypeStruct((b, m, MEM_WIDTH), BF16)] * 2,
        compiler_params=_params("parallel"),
        name="mem_kv",
    )(mem, g, w_kv, k_g)


def _mem_xattn_kernel(x_ref, g_ref, wq_ref, qg_ref, k_ref, v_ref, wo_ref, gnext_ref, o_ref, on_ref):
    x = x_ref[...]
    q = _dot(_rms(x, g_ref[...]).astype(BF16), wq_ref[...])
    qg = qg_ref[...]
    heads = []
    for h in range(MEM_HEADS):
        lo = h * MEM_HEAD_DIM
        qh = _rms(q[:, lo:lo + MEM_HEAD_DIM], qg).astype(BF16)
        s = _dot_nt(qh, k_ref[:, lo:lo + MEM_HEAD_DIM])
        e = jnp.exp(s - jnp.max(s, axis=-1, keepdims=True))
        p = e / jnp.sum(e, axis=-1, keepdims=True)
        heads.append(_dot(p.astype(BF16), v_ref[:, lo:lo + MEM_HEAD_DIM]))
    o = jnp.concatenate(heads, axis=-1).astype(BF16)
    y = x + _dot(o, wo_ref[...])
    o_ref[...] = y
    on_ref[...] = _rms(y, gnext_ref[...]).astype(BF16)


def mem_xattn(x, g, w_q, q_g_scaled, k, v, w_o, g_next, tm):
    b, l, d = x.shape
    m = k.shape[1]
    spec_x = pl.BlockSpec((None, tm, d), lambda i, j: (i, j, 0))
    spec_kv = pl.BlockSpec((None, m, MEM_WIDTH), lambda i, j: (i, 0, 0))
    return pl.pallas_call(
        _mem_xattn_kernel,
        grid=(b, l // tm),
        in_specs=[
            spec_x,
            _resident((1, d)),
            _resident((d, MEM_WIDTH)),
            _resident((1, MEM_HEAD_DIM)),
            spec_kv,
            spec_kv,
            _resident((MEM_WIDTH, d)),
            _resident((1, d)),
        ],
        out_specs=[spec_x, spec_x],
        out_shape=[jax.ShapeDtypeStruct((b, l, d), F32), jax.ShapeDtypeStruct((b, l, d), BF16)],
        compiler_params=_params("parallel", "parallel"),
        name="mem_xattn",
    )(x, g, w_q, q_g_scaled, k, v, w_o, g_next)


def _mla_pre_kernel(x_ref, gmix_ref, win_ref, qag_ref, kvag_ref, wqt_ref, wk_ref, wvt_ref, gqt_ref, gk_ref,
                    cosk_ref, sin_lo_ref, sin_hi_ref, cosq_ref, sinq_ref, q_out, k_out, v_out):
    tm = x_ref.shape[0]
    half = QK_ROPE_DIM // 2
    xn = _rms(x_ref[...], gmix_ref[...]).astype(BF16)
    proj = _dot(xn, win_ref[...])
    ql = _rms(proj[:, :Q_LORA_RANK], qag_ref[...])
    kvl = _rms(proj[:, Q_LORA_RANK:Q_LORA_RANK + KV_LORA_RANK], kvag_ref[...])
    kr = proj[:, Q_LORA_RANK + KV_LORA_RANK:]
    kvl_t = kvl.T.astype(BF16)
    qt = _dot(wqt_ref[...], ql.T.astype(BF16))
    vt = _dot(wvt_ref[...], kvl_t)
    kn_all = _dot(kvl.astype(BF16), wk_ref[...])
    inv_dim = 1.0 / QK_HEAD_DIM

    gk = gk_ref[...]
    ss_kr = jnp.sum(kr * kr, axis=-1, keepdims=True)
    krg = kr * gk[:, QK_NOPE_DIM:]
    kr_rot = (krg * cosk_ref[...] + pltpu.roll(krg, half, 1) * sin_hi_ref[...]
              + pltpu.roll(krg, LANE - half, 1) * sin_lo_ref[...])
    reps = tm // LANE
    gqt = jnp.tile(gqt_ref[...], (1, reps))
    cos_q = cosq_ref[...]
    sin_q = sinq_ref[...]
    zero_rows = jnp.zeros((QK_PAD_DIM - QK_HEAD_DIM, tm), BF16)
    for h in range(MLA_HEADS):
        qh = qt[h * QK_PAD_DIM:h * QK_PAD_DIM + QK_HEAD_DIM, :]
        inv_q = lax.rsqrt(jnp.sum(qh * qh, axis=0, keepdims=True) * inv_dim + EPS)
        qn = qh * inv_q * gqt
        x1 = qn[QK_NOPE_DIM:QK_NOPE_DIM + half, :]
        x2 = qn[QK_NOPE_DIM + half:, :]
        q_out[h, :QK_NOPE_DIM, :] = qn[:QK_NOPE_DIM, :].astype(BF16)
        q_out[h, QK_NOPE_DIM:QK_NOPE_DIM + half, :] = (x1 * cos_q - x2 * sin_q).astype(BF16)
        q_out[h, QK_NOPE_DIM + half:QK_HEAD_DIM, :] = (x1 * sin_q + x2 * cos_q).astype(BF16)
        q_out[h, QK_HEAD_DIM:, :] = zero_rows
        kn = kn_all[:, h * QK_NOPE_DIM:(h + 1) * QK_NOPE_DIM]
        inv_k = lax.rsqrt((jnp.sum(kn * kn, axis=-1, keepdims=True) + ss_kr) * inv_dim + EPS)
        k_out[h, :, :QK_NOPE_DIM] = (kn * inv_k * gk[:, :QK_NOPE_DIM]).astype(BF16)
        k_out[h, :, QK_NOPE_DIM:] = (kr_rot * inv_k).astype(BF16)
        v_out[h] = vt[h * V_HEAD_DIM:(h + 1) * V_HEAD_DIM, :].astype(BF16)


def mla_pre(x, gmix, w_in_p, qag, kvag, w_qt, w_k, w_vt, gq_t, gk_p, rope_tabs, tm):
    b, l, d = x.shape
    h = MLA_HEADS
    cos_k, sin_lo, sin_hi, cos_q, sin_q = rope_tabs
    spec_tab_k = pl.BlockSpec((tm, LANE), lambda i, j: (j, 0))
    spec_tab_q = pl.BlockSpec((QK_ROPE_DIM // 2, tm), lambda i, j: (0, j))
    spec_qv = lambda rows: pl.BlockSpec((None, h, rows, tm), lambda i, j: (i, 0, 0, j))
    return pl.pallas_call(
        _mla_pre_kernel,
        grid=(b, l // tm),
        in_specs=[
            pl.BlockSpec((None, tm, d), lambda i, j: (i, j, 0)),
            _resident((1, d)),
            _resident(w_in_p.shape),
            _resident((1, Q_LORA_RANK)),
            _resident((1, KV_LORA_RANK)),
            _resident(w_qt.shape),
            _resident(w_k.shape),
            _resident(w_vt.shape),
            _resident(gq_t.shape),
            _resident((1, QK_PAD_DIM)),
            spec_tab_k, spec_tab_k, spec_tab_k, spec_tab_q, spec_tab_q,
        ],
        out_specs=[spec_qv(QK_PAD_DIM), pl.BlockSpec((None, h, tm, QK_PAD_DIM), lambda i, j: (i, 0, j, 0)),
                   spec_qv(V_HEAD_DIM)],
        out_shape=[
            jax.ShapeDtypeStruct((b, h, QK_PAD_DIM, l), BF16),
            jax.ShapeDtypeStruct((b, h, l, QK_PAD_DIM), BF16),
            jax.ShapeDtypeStruct((b, h, V_HEAD_DIM, l), BF16),
        ],
        compiler_params=_params("parallel", "parallel"),
        name="mla_pre",
    )(x, gmix, w_in_p, qag, kvag, w_qt, w_k, w_vt, gq_t, gk_p, cos_k, sin_lo, sin_hi, cos_q, sin_q)


def _flash_kernel(q_ref, k_ref, vt_ref, o_ref, m_ref, l_ref, acc_ref):
    kv = pl.program_id(3)

    @pl.when(kv == 0)
    def _():
        m_ref[...] = jnp.full_like(m_ref, -jnp.inf)
        l_ref[...] = jnp.zeros_like(l_ref)
        acc_ref[...] = jnp.zeros_like(acc_ref)

    s = _dot(k_ref[...], q_ref[...])
    m_prev = m_ref[...]
    m_new = jnp.maximum(m_prev, jnp.max(s, axis=0, keepdims=True))
    alpha = jnp.exp2(m_prev - m_new)
    p = jnp.exp2(s - m_new)
    l_ref[...] = alpha * l_ref[...] + jnp.sum(p, axis=0, keepdims=True)
    acc_ref[...] = alpha * acc_ref[...] + _dot(vt_ref[...], p.astype(BF16))
    m_ref[...] = m_new

    @pl.when(kv == pl.num_programs(3) - 1)
    def _():
        o_ref[...] = (acc_ref[...] / l_ref[...]).T.astype(o_ref.dtype)


def flash_attention(qt, k, vt, tq, tk):
    b, h, dk, l = qt.shape
    dv = vt.shape[2]
    return pl.pallas_call(
        _flash_kernel,
        grid=(b, h, l // tq, l // tk),
        in_specs=[
            pl.BlockSpec((None, None, dk, tq), lambda i, j, a, c: (i, j, 0, a)),
            pl.BlockSpec((None, None, tk, dk), lambda i, j, a, c: (i, j, c, 0)),
            pl.BlockSpec((None, None, dv, tk), lambda i, j, a, c: (i, j, 0, c)),
        ],
        out_specs=pl.BlockSpec((None, tq, dv), lambda i, j, a, c: (i, a, j)),
        out_shape=jax.ShapeDtypeStruct((b, l, h * dv), BF16),
        scratch_shapes=[pltpu.VMEM((1, tq), F32), pltpu.VMEM((1, tq), F32), pltpu.VMEM((dv, tq), F32)],
        compiler_params=_params("parallel", "parallel", "parallel", "arbitrary"),
        name="flash_attention",
    )(qt, k, vt)


def _flash_bounded_kernel(q_ref, k_ref, vt_ref, o_ref, acc_ref, *, tk):
    dv = vt_ref.shape[0]
    q = q_ref[...]
    ones = jnp.ones((ONES_ROWS, tk), BF16)
    for c in range(k_ref.shape[0] // tk):
        cols = slice(c * tk, (c + 1) * tk)
        p = jnp.exp2(_dot(k_ref[cols, :], q)).astype(BF16)
        pv = _dot(jnp.concatenate([vt_ref[:, cols], ones], axis=0), p)
        if c == 0:
            acc_ref[...] = pv
        else:
            acc_ref[...] += pv
    acc = acc_ref[...]
    o_ref[...] = (acc[:dv] / acc[dv:dv + 1]).T.astype(o_ref.dtype)


def flash_attention_bounded(qt, k, vt, tq, tk):
    b, h, dk, l = qt.shape
    dv = vt.shape[2]
    return pl.pallas_call(
        functools.partial(_flash_bounded_kernel, tk=tk),
        grid=(b, h, l // tq),
        in_specs=[
            pl.BlockSpec((None, None, dk, tq), lambda i, j, a: (i, j, 0, a)),
            pl.BlockSpec((None, None, l, dk), lambda i, j, a: (i, j, 0, 0)),
            pl.BlockSpec((None, None, dv, l), lambda i, j, a: (i, j, 0, 0)),
        ],
        out_specs=pl.BlockSpec((None, tq, dv), lambda i, j, a: (i, a, j)),
        out_shape=jax.ShapeDtypeStruct((b, l, h * dv), BF16),
        scratch_shapes=[pltpu.VMEM((dv + ONES_ROWS, tq), F32)],
        compiler_params=_params("parallel", "parallel", "arbitrary"),
        name="flash_attention_bounded",
    )(qt, k, vt)


def _hgrn_scan_kernel(*refs, reverse, n_chunks):
    if reverse:
        q_ref, z_ref, v_ref, lb_ref, sel_ref, of_ref, gate_ref, ong_ref, o_ref, s_ref, b_ref, kk_ref, lhs_ref = refs
    else:
        q_ref, z_ref, v_ref, lb_ref, sel_ref, o_ref, s_ref, b_ref, kk_ref, lhs_ref = refs
    c = HGRN_CHUNK
    sub = HGRN_SUB
    per = c // sub
    kd = HGRN_KEY_DIM
    tl = n_chunks * c

    @pl.when(pl.program_id(2) == 0)
    def _():
        s_ref[...] = jnp.zeros_like(s_ref)

    lb = lb_ref[...]
    z = z_ref[...]
    e = jnp.exp(-jnp.abs(z))
    r = 1.0 / (1.0 + e)
    f = lb + (1.0 - lb) * (jnp.where(z >= 0.0, 1.0, e) * r)
    log2_f = jnp.where(f > 0.0, jnp.log2(f), (z + jnp.log1p(-lb)) * LOG2_E)
    kk = (1.0 - lb) * (jnp.where(z >= 0.0, e, 1.0) * r)

    row = lax.broadcasted_iota(jnp.int32, (c, c), 0)
    col = lax.broadcasted_iota(jnp.int32, (c, c), 1)
    visible = (col >= row) if reverse else (col <= row)
    tri_f = visible.astype(F32)
    for ci in range(n_chunks):
        bsum = jnp.dot(tri_f, log2_f[ci * c:(ci + 1) * c], precision=lax.Precision.HIGHEST,
                       preferred_element_type=F32)
        b_ref[ci * per:(ci + 1) * per] = bsum.reshape(per, sub, kd)
    kk_ref[...] = kk.reshape(tl // sub, sub, kd)

    q3 = q_ref[...].reshape(tl // sub, sub, kd)
    b3 = b_ref[...]
    for s in range(sub):
        k_row = kk_ref[:, s:s + 1, :]
        b_row = b_ref[:, s:s + 1, :]
        w = q3 * k_row * jnp.exp2(jnp.minimum(b3 - b_row, 0.0))
        lhs_ref[:, s * kd:(s + 1) * kd] = w.reshape(tl, kd).astype(BF16)
    d_all = _dot(lhs_ref[...], sel_ref[...])

    q2 = q_ref[...]
    k2 = kk.reshape(tl, kd)
    b2 = b3.reshape(tl, kd)

    split_q, split_k, split_mask = [], [], []
    group = c
    while group > sub:
        half = group // 2
        pos = lax.broadcasted_iota(jnp.int32, (1, group, 1), 1)
        is_q = (pos < half) if reverse else (pos >= half)
        bg = b2.reshape(tl // group, group, kd)
        split_row = half if reverse else half - 1
        b_split = bg[:, split_row:split_row + 1, :]
        dec = jnp.exp2(jnp.where(is_q, bg - b_split, b_split - bg))
        split_q.append(jnp.where(is_q, q2.reshape(bg.shape) * dec, 0.0).astype(BF16).reshape(tl, kd))
        split_k.append(jnp.where(is_q, 0.0, k2.reshape(bg.shape) * dec).astype(BF16).reshape(tl, kd))
        split_mask.append(None if group == c else jnp.where((row // group) == (col // group), 1.0, 0.0))
        group = half
    mask_d = jnp.where(((row // sub) == (col // sub)) & visible, 1.0, 0.0)

    edge = 0 if reverse else c - 1
    bc3 = b2.reshape(n_chunks, c, kd)
    b_edge = bc3[:, edge:edge + 1, :]
    q_in = (q2.reshape(bc3.shape) * jnp.exp2(bc3)).astype(BF16)
    k_out = (k2.reshape(bc3.shape) * jnp.exp2(b_edge - bc3)).astype(BF16)
    edge_decay = jnp.exp2(b_edge)

    order = range(n_chunks - 1, -1, -1) if reverse else range(n_chunks)
    o_intra, s_add = {}, {}
    for ci in order:
        rows = slice(ci * c, (ci + 1) * c)
        att = d_all[rows, :] * mask_d
        for qs, ks, mask in zip(split_q, split_k, split_mask):
            part = _dot_nt(qs[rows, :], ks[rows, :])
            att = att + (part if mask is None else part * mask)
        vc = v_ref[rows, :]
        o_intra[ci] = _dot(att.astype(BF16), vc.astype(BF16))
        s_add[ci] = _dot(vc.T.astype(BF16), k_out[ci])

    st = s_ref[...]
    for ci in order:
        rows = slice(ci * c, (ci + 1) * c)
        o = o_intra[ci] + _dot_nt(q_in[ci], st.astype(BF16))
        st = st * edge_decay[ci] + s_add[ci]
        if reverse:
            o = _rms(o + of_ref[rows, :], ong_ref[...])
            g = gate_ref[rows, :]
            o_ref[rows, :] = (o * (g * jax.nn.sigmoid(g))).astype(o_ref.dtype)
        else:
            o_ref[rows, :] = o
    s_ref[...] = st


def hgrn_scan(proj, lb, reverse, b, l, tl, o_fwd=None, o_norm_g=None):
    h = HGRN_HEADS
    nl = l // tl
    n_chunks = tl // HGRN_CHUNK
    kd = HGRN_KEY_DIM

    def pos(i, j):
        return i * nl + ((nl - 1 - j) if reverse else j)

    def col_spec(section):
        return pl.BlockSpec((None, tl, kd), lambda i, hh, j: (section * h + hh, pos(i, j), 0))

    sel_s = lax.broadcasted_iota(jnp.int32, (HGRN_SUB, kd, HGRN_CHUNK), 0)
    sel_j = lax.broadcasted_iota(jnp.int32, (HGRN_SUB, kd, HGRN_CHUNK), 2)
    sel = (sel_j % HGRN_SUB == sel_s).astype(BF16).reshape(HGRN_SUB * kd, HGRN_CHUNK)

    in_specs = [col_spec(0), col_spec(2 if reverse else 1), col_spec(3),
                pl.BlockSpec((1, kd), lambda i, hh, j: (0, hh)), _resident(sel.shape)]
    args = [proj, proj, proj, lb, sel]
    out_spec = pl.BlockSpec((None, tl, kd), lambda i, hh, j: (hh, pos(i, j), 0))
    if reverse:
        in_specs += [out_spec, col_spec(4), _resident((1, HGRN_VAL_DIM))]
        args += [o_fwd, proj, o_norm_g]
    return pl.pallas_call(
        functools.partial(_hgrn_scan_kernel, reverse=reverse, n_chunks=n_chunks),
        grid=(b, h, nl),
        in_specs=in_specs,
        out_specs=out_spec,
        out_shape=jax.ShapeDtypeStruct((h, b * l, kd), BF16 if reverse else F32),
        scratch_shapes=[
            pltpu.VMEM((HGRN_VAL_DIM, kd), F32),
            pltpu.VMEM((tl // HGRN_SUB, HGRN_SUB, kd), F32),
            pltpu.VMEM((tl // HGRN_SUB, HGRN_SUB, kd), F32),
            pltpu.VMEM((tl, HGRN_SUB * kd), BF16),
        ],
        compiler_params=_params("parallel", "parallel", "arbitrary"),
        name="hgrn_scan_bwd" if reverse else "hgrn_scan_fwd",
    )(*args)


def _fnet_channel_kernel(x_ref, g_ref, cn_ref, sn_ref, u_ref, w_ref):
    xn = _rms(x_ref[...], g_ref[...]).astype(BF16)
    for gi in range(FNET_GROUPS):
        cols = slice(gi * FNET_GROUP_DIM, (gi + 1) * FNET_GROUP_DIM)
        u_ref[:, cols] = _dot(xn[:, cols], cn_ref[...]).astype(BF16)
        w_ref[:, cols] = _dot(xn[:, cols], sn_ref[...]).astype(BF16)


def fnet_channel(x, g, cn, sn, tm):
    t, d = x.shape
    spec = pl.BlockSpec((tm, d), lambda i: (i, 0))
    return pl.pallas_call(
        _fnet_channel_kernel,
        grid=(t // tm,),
        in_specs=[spec, _resident((1, d)), _resident(cn.shape), _resident(sn.shape)],
        out_specs=[spec, spec],
        out_shape=[jax.ShapeDtypeStruct((t, d), BF16)] * 2,
        compiler_params=_params("parallel"),
        name="fnet_channel",
    )(x, g, cn, sn)


def _fnet_stage1_kernel(m1_ref, u_ref, w_ref, are_ref, aim_ref, uf_ref, wf_ref, ar_ref, ai_ref):
    l1, n_slabs, _ = u_ref.shape
    m1 = m1_ref[...]
    uf_ref[...] = u_ref[...].astype(F32)
    wf_ref[...] = w_ref[...].astype(F32)
    for s in range(n_slabs):
        z = jnp.concatenate([uf_ref[:, s, :], wf_ref[:, s, :]], axis=0).astype(BF16)
        a = _dot(m1, z)
        ar_ref[:, s, :] = a[:l1]
        ai_ref[:, s, :] = a[l1:]
    are_ref[...] = ar_ref[...].astype(BF16)
    aim_ref[...] = ai_ref[...].astype(BF16)


def fnet_stage1(m1, u, w, l1, l2, n_slabs, cb):
    b, l, d = u.shape
    spec = pl.BlockSpec((None, l1, n_slabs, cb), lambda i, j, c: (i, 0, j, c))
    a_re, a_im = pl.pallas_call(
        _fnet_stage1_kernel,
        grid=(b, l2 // n_slabs, d // cb),
        in_specs=[_resident(m1.shape), spec, spec],
        out_specs=[spec, spec],
        out_shape=[jax.ShapeDtypeStruct((b, l1, l2, d), BF16)] * 2,
        scratch_shapes=[pltpu.VMEM((l1, n_slabs, cb), F32)] * 4,
        compiler_params=_params("parallel", "parallel", "parallel"),
        name="fnet_stage1",
    )(m1, u.reshape(b, l1, l2, d), w.reshape(b, l1, l2, d))
    return a_re, a_im


def _fnet_stage2_kernel(are_ref, aim_ref, twc_ref, tws_ref, m2_ref, x_ref, wo_ref, o_ref, y_ref):
    n_slabs, l2, d = are_ref.shape
    m2 = m2_ref[...]
    for j in range(n_slabs):
        a_re = are_ref[j].astype(F32)
        a_im = aim_ref[j].astype(F32)
        c = jnp.tile(twc_ref[j], (1, d // LANE))
        s = jnp.tile(tws_ref[j], (1, d // LANE))
        b_re = a_re * c + a_im * s
        b_im = a_im * c - a_re * s
        y = _dot(m2, jnp.concatenate([b_re, b_im], axis=0).astype(BF16))
        y_ref[j * l2:(j + 1) * l2, :] = y.astype(BF16)
    out = _dot(y_ref[...], wo_ref[...])
    for j in range(n_slabs):
        o_ref[:, j, :] = x_ref[:, j, :] + out[j * l2:(j + 1) * l2, :]


def fnet_stage2(a_re, a_im, tw_c, tw_s, m2, x, w_o, l1, l2, n_slabs):
    b, l, d = x.shape
    spec_a = pl.BlockSpec((None, n_slabs, l2, d), lambda i, j: (i, j, 0, 0))
    spec_tw = pl.BlockSpec((n_slabs, l2, LANE), lambda i, j: (j, 0, 0))
    spec_x = pl.BlockSpec((None, l2, n_slabs, d), lambda i, j: (i, 0, j, 0))
    out = pl.pallas_call(
        _fnet_stage2_kernel,
        grid=(b, l1 // n_slabs),
        in_specs=[spec_a, spec_a, spec_tw, spec_tw, _resident(m2.shape), spec_x, _resident(w_o.shape)],
        out_specs=spec_x,
        out_shape=jax.ShapeDtypeStruct((b, l2, l1, d), F32),
        scratch_shapes=[pltpu.VMEM((n_slabs * l2, d), BF16)],
        compiler_params=_params("parallel", "parallel"),
        name="fnet_stage2",
    )(a_re, a_im, tw_c, tw_s, m2, x.reshape(b, l2, l1, d), w_o)
    return out.reshape(b, l, d)


def _fnet_seq_tables(l1, l2):
    c1, s1 = _dft_tables(l1)
    c2, s2 = _dft_tables(l2)
    m1 = jnp.concatenate([jnp.concatenate([c1, -s1], axis=1), jnp.concatenate([-s1, -c1], axis=1)], axis=0)
    m2 = jnp.concatenate([c2, s2], axis=1)
    k1 = jnp.arange(l1, dtype=jnp.int32)[:, None]
    n2 = jnp.arange(l2, dtype=jnp.int32)[None, :]
    ang = ((k1 * n2) % (l1 * l2)).astype(F32)[:, :, None] * (2.0 * jnp.pi / (l1 * l2))
    tw_c = jnp.broadcast_to(jnp.cos(ang), (l1, l2, LANE))
    tw_s = jnp.broadcast_to(jnp.sin(ang), (l1, l2, LANE))
    return m1.astype(BF16), m2.astype(BF16), tw_c, tw_s


def _dft_tables(n):
    idx = jnp.arange(n, dtype=jnp.int32)
    jk = (idx[:, None] * idx[None, :]) % n
    ang = jk.astype(F32) * (2.0 * jnp.pi / n)
    scale = n ** -0.5
    return jnp.cos(ang) * scale, jnp.sin(ang) * scale


def _rope_tables(length):
    dim = QK_ROPE_DIM
    half = dim // 2
    inv = 1.0 / (ROPE_THETA ** (jnp.arange(0, dim, 2, dtype=F32) / dim))
    ang = jnp.arange(length, dtype=F32)[:, None] * inv[None, :]
    cos, sin = jnp.cos(ang), jnp.sin(ang)
    zero = jnp.zeros_like(cos)
    cos_t = jnp.concatenate([cos, cos, zero, zero], axis=-1)
    sin_lo = jnp.concatenate([-sin, zero, zero, zero], axis=-1)
    sin_hi = jnp.concatenate([zero, sin, zero, zero], axis=-1)
    return cos_t, sin_lo, sin_hi, cos.T, sin.T


def _pad_cols(a, n):
    return jnp.concatenate([a, jnp.zeros(a.shape[:-1] + (n,), a.dtype)], axis=-1)


def _row(v):
    return v.reshape(1, -1).astype(F32)


class _Tiles(NamedTuple):
    token: int
    mla_pre: int
    attn_q: int
    attn_k: int
    scan: int
    hgrn_cols: int
    dft_inner: int
    dft_slabs1: int
    dft_cols1: int
    dft_slabs2: int
    ff: int


def _tiles(l):
    dft_inner = 64 if l >= 1024 else 16
    return _Tiles(token=min(l, 512), mla_pre=min(l, 256), attn_q=min(l, 2048), attn_k=min(l, 1024),
                  scan=min(l, 2048),
                  hgrn_cols=2048, dft_inner=dft_inner, dft_slabs1=16, dft_cols1=1024,
                  dft_slabs2=min(8, l // dft_inner), ff=1024)


def kernel(x_prompt, x_sample, mem_prompt, mem_sample, norm_mix_g, mla_w_in, mla_q_a_norm_g, mla_kv_a_norm_g, mla_w_q_b, mla_w_kv_b, mla_q_norm_g, mla_k_norm_g, mla_w_o, hgrn_w_in, hgrn_lb_logits, hgrn_o_norm_g, hgrn_w_o, fnet_w_o, norm_xq_g, norm_mem_g, mem_w_q, mem_w_kv, mem_q_norm_g, mem_k_norm_g, mem_w_o, norm_mlp_g, mlp_w1, mlp_w2):
    depth = norm_mix_g.shape[0]
    n_mixers = 3

    lb_all = jnp.cumsum(jax.nn.softmax(hgrn_lb_logits.astype(F32), axis=0), axis=0)
    lb_all = lb_all - lb_all[:1]

    mla_w_in_p = _pad_cols(mla_w_in, LANE - QK_ROPE_DIM).astype(BF16)
    n_mla = mla_w_in.shape[0]
    wqb = mla_w_q_b.reshape(n_mla, Q_LORA_RANK, MLA_HEADS, QK_HEAD_DIM)
    mla_w_qb_p = _pad_cols(wqb, QK_PAD_DIM - QK_HEAD_DIM).reshape(n_mla, Q_LORA_RANK, MLA_HEADS * QK_PAD_DIM).astype(BF16)
    mla_w_qt = jnp.swapaxes(mla_w_qb_p, 1, 2)
    kvb = mla_w_kv_b.reshape(n_mla, KV_LORA_RANK, MLA_HEADS, QK_NOPE_DIM + V_HEAD_DIM)
    mla_w_k = kvb[..., :QK_NOPE_DIM].reshape(n_mla, KV_LORA_RANK, MLA_HEADS * QK_NOPE_DIM).astype(BF16)
    mla_w_vt = jnp.swapaxes(kvb[..., QK_NOPE_DIM:].reshape(n_mla, KV_LORA_RANK, MLA_HEADS * V_HEAD_DIM),
                            1, 2).astype(BF16)
    mla_gq_t = jnp.broadcast_to((mla_q_norm_g * (QK_HEAD_DIM ** -0.5 * LOG2_E))[:, :, None],
                                (n_mla, QK_HEAD_DIM, LANE))
    mla_score_bound = (QK_HEAD_DIM ** 0.5 * LOG2_E * 1.02) * (
        jnp.max(jnp.abs(mla_q_norm_g), axis=-1) * jnp.max(jnp.abs(mla_k_norm_g), axis=-1))
    mla_gk_p = _pad_cols(mla_k_norm_g, QK_PAD_DIM - QK_HEAD_DIM)
    mla_w_o_b = mla_w_o.astype(BF16)
    hgrn_w_in_b = hgrn_w_in.astype(BF16)
    hgrn_w_o_b = hgrn_w_o.astype(BF16)
    fnet_w_o_b = fnet_w_o.astype(BF16)
    mem_w_q_b = mem_w_q.astype(BF16)
    mem_w_kv_b = mem_w_kv.astype(BF16)
    mem_w_o_b = mem_w_o.astype(BF16)
    mem_qg_scaled = mem_q_norm_g * (MEM_HEAD_DIM ** -0.5)
    mlp_w1_b = mlp_w1.astype(BF16)
    mlp_w2_b = mlp_w2.astype(BF16)

    has_fnet = depth >= n_mixers
    if has_fnet:
        cn, sn = _dft_tables(FNET_GROUP_DIM)
        cn, sn = cn.astype(BF16), sn.astype(BF16)

    def run_trunk(x, mem):
        b, l, d = x.shape
        t = b * l
        tiles = _tiles(l)
        tm = tiles.token
        if depth >= 1:
            rope_tabs = _rope_tables(l)
        if has_fnet:
            l2 = tiles.dft_inner
            l1 = l // l2
            m1, m2, tw_c, tw_s = _fnet_seq_tables(l1, l2)
        for i in range(depth):
            kind = i % n_mixers
            j = i // n_mixers
            gmix = _row(norm_mix_g[i])
            if kind == 0:
                q, k, v = mla_pre(x, gmix, mla_w_in_p[j], _row(mla_q_a_norm_g[j]), _row(mla_kv_a_norm_g[j]),
                                  mla_w_qt[j], mla_w_k[j], mla_w_vt[j], mla_gq_t[j], _row(mla_gk_p[j]),
                                  rope_tabs, tm=tiles.mla_pre)
                o = lax.cond(mla_score_bound[j] <= SCORE_LOG2_LIMIT,
                             functools.partial(flash_attention_bounded, tq=tiles.attn_q, tk=tiles.attn_k),
                             functools.partial(flash_attention, tq=tiles.attn_k, tk=tiles.attn_k), q, k, v)
                x = matmul_residual(o.reshape(t, d), mla_w_o_b[j], x.reshape(t, d), tm).reshape(b, l, d)
            elif kind == 1:
                proj = norm_matmul_colmajor(x.reshape(t, d), gmix, hgrn_w_in_b[j], F32, tm, tiles.hgrn_cols)
                tl = tiles.scan
                o_f = hgrn_scan(proj, _row(lb_all[i, 0]), False, b, l, tl)
                y = hgrn_scan(proj, _row(lb_all[i, 1]), True, b, l, tl, o_fwd=o_f,
                              o_norm_g=_row(hgrn_o_norm_g[j]))
                x = matmul_residual(y, hgrn_w_o_b[j], x.reshape(t, d), tm).reshape(b, l, d)
            else:
                u, w = fnet_channel(x.reshape(t, d), gmix, cn, sn, tm)
                a_re, a_im = fnet_stage1(m1, u.reshape(b, l, d), w.reshape(b, l, d), l1, l2,
                                         tiles.dft_slabs1, tiles.dft_cols1)
                x = fnet_stage2(a_re, a_im, tw_c, tw_s, m2, x, fnet_w_o_b[j], l1, l2, tiles.dft_slabs2)
            mk, mv = mem_kv(mem, _row(norm_mem_g[i]), mem_w_kv_b[i], _row(mem_k_norm_g[i]))
            x, xn = mem_xattn(x, _row(norm_xq_g[i]), mem_w_q_b[i], _row(mem_qg_scaled[i]), mk, mv, mem_w_o_b[i],
                              _row(norm_mlp_g[i]), tm)
            x = mlp(x.reshape(t, d), xn.reshape(t, d), mlp_w1_b[i], mlp_w2_b[i], tm, tiles.ff).reshape(b, l, d)
        return x

    return (run_trunk(x_prompt, mem_prompt), run_trunk(x_sample, mem_sample))
```
